```python
import math
import jax
import jax.numpy as jnp
from jax import lax
import numpy as np

D_MODEL = 2048
BATCH = 2
SEQ = 4096
DEPTH = 4
DEC_BATCH = 8
DEC_SEQ = 4
PAST_LEN = 16384
PAGE_SIZE = 128

N_MIXERS = 2
N_GDN = (DEPTH + 1) // 2
N_ATTN = DEPTH // 2
GDN_DK = 128
GDN_DV = 128
GDN_HK = D_MODEL // 128
GDN_HV = 2 * GDN_HK
GDN_QK_DIM = GDN_HK * GDN_DK
GDN_V_DIM = GDN_HV * GDN_DV
GDN_CONV_DIM = 2 * GDN_QK_DIM + GDN_V_DIM
GDN_IN_DIM = GDN_CONV_DIM + GDN_V_DIM + 2 * GDN_HV
CONV_W = 4
CHUNK = 64
DA_DH = 128
DA_H = D_MODEL // (2 * DA_DH)
DA_QK_DIM = DA_H * 2 * DA_DH
DA_V_DIM = DA_H * 2 * DA_DH
Q_BLOCK = 128
ROPE_THETA = 10000.0
MEM_LEN = 256
X_H = 4
X_DH = 128
D_FF = 4 * D_MODEL
EPS = 1e-6

kernel_name = "hybrid_gdn_diffattn_decoder_step"

F32 = jnp.float32


def rms_norm(x, gain, eps=EPS):
    xf = x.astype(F32)
    y = xf * lax.rsqrt(jnp.mean(xf * xf, axis=-1, keepdims=True) + eps)
    return (y * gain.astype(F32)).astype(x.dtype)


def l2_normalize(x):
    return x * lax.rsqrt(jnp.sum(x * x, axis=-1, keepdims=True) + 1e-6)


def rope(x, pos):
    half = DA_DH // 2
    inv = 1.0 / (ROPE_THETA ** (jnp.arange(half, dtype=F32) * (2.0 / DA_DH)))
    ang = pos.astype(F32)[:, None] * inv[None, :]
    cos = jnp.cos(ang)[:, None, None, :]
    sin = jnp.sin(ang)[:, None, None, :]
    xf = x.astype(F32)
    x1, x2 = xf[..., :half], xf[..., half:]
    return jnp.concatenate([x1 * cos - x2 * sin, x2 * cos + x1 * sin], axis=-1).astype(x.dtype)


def short_conv(u, buf, w):
    L = u.shape[1]
    xc = jnp.concatenate([buf.astype(u.dtype), u], axis=1)
    y = xc[:, 0:L] * w[0]
    for i in range(1, CONV_W):
        y = y + xc[:, i:i + L] * w[i]
    return jax.nn.silu(y), xc[:, L:]


def gated_delta_chunked(q, k, v, beta, g, S0):
    Bsz, L, H, DK = q.shape
    DV = v.shape[-1]
    C = min(CHUNK, L)
    pad = (-L) % C
    if pad:
        q = jnp.pad(q, ((0, 0), (0, pad), (0, 0), (0, 0)))
        k = jnp.pad(k, ((0, 0), (0, pad), (0, 0), (0, 0)))
        v = jnp.pad(v, ((0, 0), (0, pad), (0, 0), (0, 0)))
        beta = jnp.pad(beta, ((0, 0), (0, pad), (0, 0)))
        g = jnp.pad(g, ((0, 0), (0, pad), (0, 0)))
    NC = (L + pad) // C

    def chunks(t):
        t = t.reshape((Bsz, NC, C) + t.shape[2:])
        return jnp.moveaxis(t, (1, 3), (0, 2))

    qc, kc, vc, bc = chunks(q), chunks(k), chunks(v), chunks(beta)
    gcum = jnp.cumsum(chunks(g), axis=-1)
    t_idx = jnp.arange(C)
    incl = t_idx[:, None] >= t_idx[None, :]
    strict = t_idx[:, None] > t_idx[None, :]
    diff = gcum[..., :, None] - gcum[..., None, :]
    decay_incl = jnp.exp(jnp.where(incl, diff, -jnp.inf))
    decay_strict = jnp.where(strict, decay_incl, 0.0)
    a_mat = bc[..., :, None] * jnp.einsum('nbhtd,nbhjd->nbhtj', kc, kc) * decay_strict
    lhs = a_mat + jnp.eye(C, dtype=a_mat.dtype)
    rhs = jnp.concatenate([bc[..., None] * vc, (bc * jnp.exp(gcum))[..., None] * kc], axis=-1)
    sol = lax.linalg.triangular_solve(lhs, rhs, left_side=True, lower=True, unit_diagonal=True)
    u_pre, w_mat = sol[..., :DV], sol[..., DV:]
    qk = jnp.einsum('nbhtd,nbhjd->nbhtj', qc, kc) * decay_incl
    g_last = gcum[..., -1]
    k_tail = kc * jnp.exp(g_last[..., None] - gcum)[..., None]

    def step(S, inp):
        qi, qki, ui_pre, wi, gi, gl, kti = inp
        u = ui_pre - jnp.einsum('bhtd,bhde->bhte', wi, S)
        o = jnp.exp(gi)[..., None] * jnp.einsum('bhtd,bhde->bhte', qi, S) + jnp.einsum('bhtj,bhje->bhte', qki, u)
        S = jnp.exp(gl)[..., None, None] * S + jnp.einsum('bhtd,bhte->bhde', kti, u)
        return S, o

    S, o = lax.scan(step, S0, (qc, qk, u_pre, w_mat, gcum, g_last, k_tail))
    o = jnp.moveaxis(o, (0, 2), (1, 3)).reshape(Bsz, NC * C, H, DV)[:, :L]
    return o, S


def gdn_mixer(xn, conv_buf, S0, w_in, conv_w, a_log, dt_bias, norm_g, w_out):
    Bsz, L, _ = xn.shape
    h = xn @ w_in
    qkv, z, b, a = jnp.split(h, [GDN_CONV_DIM, GDN_CONV_DIM + GDN_V_DIM, GDN_CONV_DIM + GDN_V_DIM + GDN_HV], axis=-1)
    qkv_c, new_buf = short_conv(qkv, conv_buf, conv_w)
    q, k, v = jnp.split(qkv_c.astype(F32), [GDN_QK_DIM, 2 * GDN_QK_DIM], axis=-1)
    rep = GDN_HV // GDN_HK
    q = jnp.repeat(q.reshape(Bsz, L, GDN_HK, GDN_DK), rep, axis=2)
    k = jnp.repeat(k.reshape(Bsz, L, GDN_HK, GDN_DK), rep, axis=2)
    q = l2_normalize(q) * (GDN_DK ** -0.5)
    k = l2_normalize(k)
    v = v.reshape(Bsz, L, GDN_HV, GDN_DV)
    beta = jax.nn.sigmoid(b.astype(F32))
    g = -jnp.exp(a_log.astype(F32)) * jax.nn.softplus(a.astype(F32) + dt_bias.astype(F32))
    o, S = gated_delta_chunked(q, k, v, beta, g, S0.astype(F32))
    o = rms_norm(o, norm_g) * jax.nn.silu(z.astype(F32).reshape(Bsz, L, GDN_HV, GDN_DV))
    y = o.reshape(Bsz, L, GDN_V_DIM).astype(xn.dtype) @ w_out
    return y, new_buf, S


def diff_lambda(lam, lam_init):
    lf = lam.astype(F32)
    return jnp.exp(jnp.sum(lf[0] * lf[1])) - jnp.exp(jnp.sum(lf[2] * lf[3])) + lam_init


def diff_attn_project(xn, pos, w_qkv):
    Bsz, L, _ = xn.shape
    q, k, v = jnp.split(xn @ w_qkv, 3, axis=-1)
    q = rope(q.reshape(Bsz, L, DA_H, 2, DA_DH), pos)
    k = rope(k.reshape(Bsz, L, DA_H, 2, DA_DH), pos)
    v = v.reshape(Bsz, L, DA_H, 2 * DA_DH)
    return q, k, v


def diff_attn_prompt(q, k, v, lam):
    Bsz, L = q.shape[:2]
    blk = min(Q_BLOCK, L)
    nb = L // blk
    scale = DA_DH ** -0.5
    qb = jnp.moveaxis(q.reshape(Bsz, nb, blk, DA_H, 2, DA_DH), 1, 0)
    vf = v.astype(F32)
    kpos = jnp.arange(L)

    def block(args):
        qi, start = args
        s = jnp.einsum('bqhcd,bkhcd->bhcqk', qi, k, preferred_element_type=F32) * scale
        qpos = start + jnp.arange(blk)
        s = jnp.where(kpos[None, :] <= qpos[:, None], s, -jnp.inf)
        p = jax.nn.softmax(s, axis=-1)
        pd = p[:, :, 0] - lam * p[:, :, 1]
        return jnp.einsum('bhqk,bkhe->bqhe', pd, vf)

    o = lax.map(block, (qb, jnp.arange(nb) * blk))
    return jnp.moveaxis(o, 0, 1).reshape(Bsz, L, DA_H, 2 * DA_DH)


def diff_attn_sample(q, k, v, k_pool, v_pool, page_table, lam):
    T = q.shape[1]
    qf = q.astype(F32) * (DA_DH ** -0.5)
    s = jnp.einsum('bqhcd,bkhcd->bhcqk', qf, k.astype(F32))
    causal = jnp.arange(T)[None, :] <= jnp.arange(T)[:, None]
    s = jnp.where(causal, s, -jnp.inf)
    m = jnp.max(s, axis=-1)
    e = jnp.exp(s - m[..., None])
    l = jnp.sum(e, axis=-1)
    acc = jnp.einsum('bhcqk,bkhe->bhcqe', e, v.astype(F32))

    def page_step(carry, phys):
        m, l, acc = carry
        kp = k_pool[phys].astype(F32)
        vp = v_pool[phys].astype(F32)
        s = jnp.einsum('bqhcd,bkhcd->bhcqk', qf, kp)
        m_new = jnp.maximum(m, jnp.max(s, axis=-1))
        corr = jnp.exp(m - m_new)
        e = jnp.exp(s - m_new[..., None])
        l = l * corr + jnp.sum(e, axis=-1)
        acc = acc * corr[..., None] + jnp.einsum('bhcqk,bkhe->bhcqe', e, vp)
        return (m_new, l, acc), None

    (m, l, acc), _ = lax.scan(page_step, (m, l, acc), page_table.T)
    p_out = acc / l[..., None]
    o = p_out[:, :, 0] - lam * p_out[:, :, 1]
    return jnp.transpose(o, (0, 2, 1, 3))


def diff_attn_out(o, subln, lam_init, w_o, dtype):
    Bsz, L = o.shape[:2]
    o = rms_norm(o, subln, 1e-5) * (1.0 - lam_init)
    return o.reshape(Bsz, L, DA_V_DIM).astype(dtype) @ w_o


def mem_kv(mem, norm_g, w_xkv):
    Bsz, M, _ = mem.shape
    k, v = jnp.split(rms_norm(mem, norm_g) @ w_xkv, 2, axis=-1)
    return k.reshape(Bsz, M, X_H, X_DH), v.reshape(Bsz, M, X_H, X_DH)


def mem_attend(xn, mk, mv, w_xq, w_xo):
    Bsz, L, _ = xn.shape
    q = (xn @ w_xq).reshape(Bsz, L, X_H, X_DH)
    s = jnp.einsum('bqhd,bkhd->bhqk', q, mk, preferred_element_type=F32) * (X_DH ** -0.5)
    p = jax.nn.softmax(s, axis=-1)
    o = jnp.einsum('bhqk,bkhd->bqhd', p, mv.astype(F32))
    return o.reshape(Bsz, L, X_H * X_DH).astype(xn.dtype) @ w_xo


def sq_relu_mlp(xn, w_up, w_down):
    return jnp.square(jax.nn.relu(xn @ w_up)) @ w_down


def setup_inputs(seed: int = 0) -> dict:
    key = jax.random.key(seed)
    ks = iter(jax.random.split(key, 48))

    def nrm(shape, scale):
        return jax.random.normal(next(ks), shape, F32) * scale

    def gain(shape):
        return 1.0 + nrm(shape, 0.02)

    n_pages = PAST_LEN // PAGE_SIZE
    n_used = DEC_BATCH * n_pages
    n_pool = n_used + (n_used + 3) // 4
    perm = jax.random.permutation(next(ks), n_pool)
    page_table = perm[:n_used].reshape(DEC_BATCH, n_pages).astype(jnp.int32)

    dt = jnp.exp(jax.random.uniform(next(ks), (N_GDN, GDN_HV), F32, math.log(1e-3), math.log(1e-1)))
    inp = {}
    inp['x_prompt'] = nrm((BATCH, SEQ, D_MODEL), 1.0)
    inp['x_sample'] = nrm((DEC_BATCH, DEC_SEQ, D_MODEL), 1.0)
    inp['mem_prompt'] = nrm((BATCH, MEM_LEN, D_MODEL), 1.0)
    inp['state_conv'] = nrm((N_GDN, DEC_BATCH, CONV_W - 1, GDN_CONV_DIM), 1.0)
    inp['state_delta'] = nrm((N_GDN, DEC_BATCH, GDN_HV, GDN_DK, GDN_DV), GDN_DK ** -0.5)
    inp['cache_k'] = nrm((N_ATTN, n_pool, PAGE_SIZE, DA_H, 2, DA_DH), 1.0)
    inp['cache_v'] = nrm((N_ATTN, n_pool, PAGE_SIZE, DA_H, 2 * DA_DH), 1.0)
    inp['cache_mem_k'] = nrm((DEPTH, DEC_BATCH, MEM_LEN, X_H, X_DH), 1.0)
    inp['cache_mem_v'] = nrm((DEPTH, DEC_BATCH, MEM_LEN, X_H, X_DH), 1.0)
    inp['page_table'] = page_table
    inp['norm_mix'] = gain((DEPTH, D_MODEL))
    inp['gdn_w_in'] = nrm((N_GDN, D_MODEL, GDN_IN_DIM), D_MODEL ** -0.5)
    inp['gdn_conv_w'] = nrm((N_GDN, CONV_W, GDN_CONV_DIM), CONV_W ** -0.5)
    inp['gdn_a_log'] = jnp.log(jax.random.uniform(next(ks), (N_GDN, GDN_HV), F32, 1.0, 16.0))
    inp['gdn_dt_bias'] = dt + jnp.log(-jnp.expm1(-dt))
    inp['gdn_norm'] = gain((N_GDN, GDN_DV))
    inp['gdn_w_out'] = nrm((N_GDN, GDN_V_DIM, D_MODEL), GDN_V_DIM ** -0.5)
    inp['attn_w_qkv'] = nrm((N_ATTN, D_MODEL, 2 * DA_QK_DIM + DA_V_DIM), D_MODEL ** -0.5)
    inp['attn_lambda'] = nrm((N_ATTN, 4, DA_DH), 0.1)
    inp['attn_subln'] = gain((N_ATTN, 2 * DA_DH))
    inp['attn_w_o'] = nrm((N_ATTN, DA_V_DIM, D_MODEL), DA_V_DIM ** -0.5)
    inp['norm_xattn'] = gain((DEPTH, D_MODEL))
    inp['norm_mem'] = gain((DEPTH, D_MODEL))
    inp['w_xq'] = nrm((DEPTH, D_MODEL, X_H * X_DH), D_MODEL ** -0.5)
    inp['w_xkv'] = nrm((DEPTH, D_MODEL, 2 * X_H * X_DH), D_MODEL ** -0.5)
    inp['w_xo'] = nrm((DEPTH, X_H * X_DH, D_MODEL), (X_H * X_DH) ** -0.5)
    inp['norm_ffn'] = gain((DEPTH, D_MODEL))
    inp['w_up'] = nrm((DEPTH, D_MODEL, D_FF), D_MODEL ** -0.5)
    inp['w_down'] = nrm((DEPTH, D_FF, D_MODEL), D_FF ** -0.5)
    inp['norm_final'] = gain((D_MODEL,))
    return inp


def reference(x_prompt, x_sample, mem_prompt, state_conv, state_delta, cache_k, cache_v, cache_mem_k, cache_mem_v,
              page_table, norm_mix, gdn_w_in, gdn_conv_w, gdn_a_log, gdn_dt_bias, gdn_norm, gdn_w_out,
              attn_w_qkv, attn_lambda, attn_subln, attn_w_o, norm_xattn, norm_mem, w_xq, w_xkv, w_xo,
              norm_ffn, w_up, w_down, norm_final):
    xp, xs = x_prompt, x_sample
    Bp, Lp = xp.shape[0], xp.shape[1]
    Ls = xs.shape[1]
    past_len = page_table.shape[1] * cache_k.shape[2]
    pos_p = jnp.arange(Lp)
    pos_s = past_len + jnp.arange(Ls)

    p_conv, p_delta, s_conv, s_delta = [], [], [], []
    p_k, p_v, s_k, s_v = [], [], [], []
    p_mk, p_mv = [], []
    for i in range(DEPTH):
        j = i // N_MIXERS
        hp = rms_norm(xp, norm_mix[i])
        hs = rms_norm(xs, norm_mix[i])
        if i % N_MIXERS == 0:
            w = (gdn_w_in[j], gdn_conv_w[j], gdn_a_log[j], gdn_dt_bias[j], gdn_norm[j], gdn_w_out[j])
            zero_buf = jnp.zeros((Bp, CONV_W - 1, GDN_CONV_DIM), xp.dtype)
            zero_S = jnp.zeros((Bp, GDN_HV, GDN_DK, GDN_DV), F32)
            yp, cbp, Sp = gdn_mixer(hp, zero_buf, zero_S, *w)
            ys, cbs, Ss = gdn_mixer(hs, state_conv[j], state_delta[j], *w)
            p_conv.append(cbp.astype(state_conv.dtype))
            p_delta.append(Sp.astype(state_delta.dtype))
            s_conv.append(cbs.astype(state_conv.dtype))
            s_delta.append(Ss.astype(state_delta.dtype))
        else:
            lam_init = 0.8 - 0.6 * math.exp(-0.3 * i)
            lam = diff_lambda(attn_lambda[j], lam_init)
            qp, kp, vp = diff_attn_project(hp, pos_p, attn_w_qkv[j])
            yp = diff_attn_out(diff_attn_prompt(qp, kp, vp, lam), attn_subln[j], lam_init, attn_w_o[j], xp.dtype)
            qs, ks_, vs = diff_attn_project(hs, pos_s, attn_w_qkv[j])
            os_ = diff_attn_sample(qs, ks_, vs, cache_k[j], cache_v[j], page_table, lam)
            ys = diff_attn_out(os_, attn_subln[j], lam_init, attn_w_o[j], xs.dtype)
            p_k.append(kp.astype(cache_k.dtype))
            p_v.append(vp.astype(cache_v.dtype))
            s_k.append(ks_.astype(cache_k.dtype))
            s_v.append(vs.astype(cache_v.dtype))
        xp = xp + yp
        xs = xs + ys
        mk, mv = mem_kv(mem_prompt, norm_mem[i], w_xkv[i])
        xp = xp + mem_attend(rms_norm(xp, norm_xattn[i]), mk, mv, w_xq[i], w_xo[i])
        xs = xs + mem_attend(rms_norm(xs, norm_xattn[i]), cache_mem_k[i], cache_mem_v[i], w_xq[i], w_xo[i])
        p_mk.append(mk.astype(cache_mem_k.dtype))
        p_mv.append(mv.astype(cache_mem_v.dtype))
        xp = xp + sq_relu_mlp(rms_norm(xp, norm_ffn[i]), w_up[i], w_down[i])
        xs = xs + sq_relu_mlp(rms_norm(xs, norm_ffn[i]), w_up[i], w_down[i])

    y_prompt = rms_norm(xp, norm_final)
    y_sample = rms_norm(xs, norm_final)
    return (y_prompt, y_sample,
            jnp.stack(p_conv), jnp.stack(p_delta), jnp.stack(s_conv), jnp.stack(s_delta),
            jnp.stack(p_k), jnp.stack(p_v), jnp.stack(s_k), jnp.stack(s_v),
            jnp.stack(p_mk), jnp.stack(p_mv))
```

```python
import functools
import math

import jax
import jax.numpy as jnp
from jax import lax
from jax.experimental import pallas as pl
from jax.experimental.pallas import tpu as pltpu

F32 = jnp.float32
BF16 = jnp.bfloat16

LANES = 128
CHUNK = 64
CONV_W = 4
EPS = 1e-6
ROPE_THETA = 10000.0
VMEM_CAP = 60 * 1024 * 1024
SLOT_ROWS = 16


def _cparams(sem, vmem_mb):
    return pltpu.CompilerParams(dimension_semantics=sem, vmem_limit_bytes=min(VMEM_CAP, vmem_mb * 1024 * 1024))


def _rmsnorm_kernel(x_ref, g_ref, o_ref, *, eps):
    x = x_ref[...]
    y = x * lax.rsqrt(jnp.mean(x * x, axis=-1, keepdims=True) + eps)
    o_ref[...] = (y * g_ref[...]).astype(o_ref.dtype)


def rmsnorm(x, gain, out_dtype, eps=EPS):
    M, D = x.shape
    tm = min(M, 512)
    return pl.pallas_call(
        functools.partial(_rmsnorm_kernel, eps=eps),
        grid=(M // tm,),
        in_specs=[pl.BlockSpec((tm, D), lambda i: (i, 0)), pl.BlockSpec((1, D), lambda i: (0, 0))],
        out_specs=pl.BlockSpec((tm, D), lambda i: (i, 0)),
        out_shape=jax.ShapeDtypeStruct((M, D), out_dtype),
        compiler_params=_cparams(("parallel",), 32),
        name="rmsnorm",
    )(x, gain.reshape(1, D))


def _mm_kernel(*refs, nk, act, has_res):
    if has_res:
        a_ref, w_ref, r_ref, o_ref = refs[:4]
        scratch = refs[4:]
    else:
        a_ref, w_ref, o_ref = refs[:3]
        r_ref = None
        scratch = refs[3:]
    p = jnp.dot(a_ref[...].astype(BF16), w_ref[...].astype(BF16), preferred_element_type=F32)

    def finish(acc):
        if act == "relu2":
            r = jnp.maximum(acc, 0.0)
            acc = r * r
        if has_res:
            acc = acc + r_ref[...]
        o_ref[...] = acc.astype(o_ref.dtype)

    if nk == 1:
        finish(p)
    else:
        acc_ref = scratch[0]
        k = pl.program_id(2)

        @pl.when(k == 0)
        def _():
            acc_ref[...] = p

        @pl.when(k > 0)
        def _():
            acc_ref[...] += p

        @pl.when(k == nk - 1)
        def _():
            finish(acc_ref[...])


def matmul(a, w, layer, *, n_out=None, out_dtype=F32, act=None, res=None):
    M, K = a.shape
    N = n_out if n_out is not None else w.shape[2]
    if M >= 2048:
        tm, tn = 2048, min(N, 512)
    else:
        tm, tn = M, min(N, 1024)
    tk = K if K <= 2048 else 1024
    nk = K // tk
    in_specs = [pl.BlockSpec((tm, tk), lambda i, j, k: (i, k)),
                pl.BlockSpec((None, tk, tn), lambda i, j, k: (layer, k, j))]
    args = [a, w]
    if res is not None:
        in_specs.append(pl.BlockSpec((tm, tn), lambda i, j, k: (i, j)))
        args.append(res)
    out_b = jnp.dtype(out_dtype).itemsize
    est = 2 * (tm * tk * a.dtype.itemsize + tk * tn * 4 + tm * tn * out_b + (tm * tn * 4 if res is not None else 0))
    est += tm * tn * 4 * (2 if nk > 1 else 1) + tk * tn * 2
    return pl.pallas_call(
        functools.partial(_mm_kernel, nk=nk, act=act, has_res=res is not None),
        grid=(M // tm, N // tn, nk),
        in_specs=in_specs,
        out_specs=pl.BlockSpec((tm, tn), lambda i, j, k: (i, j)),
        out_shape=jax.ShapeDtypeStruct((M, N), out_dtype),
        scratch_shapes=[pltpu.VMEM((tm, tn), F32)] if nk > 1 else [],
        compiler_params=_cparams(("parallel", "parallel", "arbitrary"), est // (1024 * 1024) + 8),
        name="matmul",
    )(*args)


def _l2_groups(y, scale):
    outs = []
    for g in range(y.shape[-1] // LANES):
        yg = y[:, g * LANES:(g + 1) * LANES]
        ss = jnp.sum(yg * yg, axis=-1, keepdims=True)
        outs.append(yg * (lax.rsqrt(ss + 1e-6) * scale))
    return outs


def _conv_post(y, o_ref, rows, j, n_q, n_k, q_scale):
    y = y * jax.nn.sigmoid(y)

    @pl.when(j < n_q)
    def _():
        for g, yg in enumerate(_l2_groups(y, q_scale)):
            o_ref[rows, g * LANES:(g + 1) * LANES] = yg

    @pl.when(jnp.logical_and(j >= n_q, j < n_q + n_k))
    def _():
        for g, yg in enumerate(_l2_groups(y, 1.0)):
            o_ref[rows, g * LANES:(g + 1) * LANES] = yg

    @pl.when(j >= n_q + n_k)
    def _():
        o_ref[rows, :] = y


def _conv_prompt_kernel(x_ref, w_ref, o_ref, *, L, TR, n_q, n_k, q_scale):
    j = pl.program_id(1)
    w = w_ref[...]

    def body(i, carry):
        r0 = pl.multiple_of(i * TR, TR)
        xa = x_ref[pl.ds(r0, TR), :]
        pstart = pl.multiple_of(jnp.maximum(r0 - 8, 0), 8)
        xp = jnp.where(i == 0, 0.0, x_ref[pl.ds(pstart, 8), :])
        xx = jnp.concatenate([xp, xa], axis=0)
        y = xa * w[3:4, :]
        for s in range(1, CONV_W):
            y = y + pltpu.roll(xx, s, axis=0)[8:, :] * w[3 - s:4 - s, :]
        _conv_post(y, o_ref, pl.ds(r0, TR), j, n_q, n_k, q_scale)
        return carry

    lax.fori_loop(0, L // TR, body, 0)


def conv_prompt(h, conv_w, B, L, n_qk_ch, n_ch, q_scale):
    tc = 256
    n_q = n_qk_ch // tc
    return pl.pallas_call(
        functools.partial(_conv_prompt_kernel, L=L, TR=256, n_q=n_q, n_k=n_q, q_scale=q_scale),
        grid=(B, n_ch // tc),
        in_specs=[pl.BlockSpec((L, tc), lambda b, j: (b, j)), pl.BlockSpec((CONV_W, tc), lambda b, j: (0, j))],
        out_specs=pl.BlockSpec((L, tc), lambda b, j: (b, j)),
        out_shape=jax.ShapeDtypeStruct((B * L, n_ch), F32),
        compiler_params=_cparams(("parallel", "parallel"), 40),
        name="conv_prompt",
    )(h, conv_w)


def _conv_sample_kernel(u_ref, st_ref, w_ref, o_ref, *, T, n_q, n_k, q_scale):
    j = pl.program_id(0)
    w = w_ref[...]
    xc = [st_ref[i] for i in range(CONV_W - 1)] + [u_ref[t] for t in range(T)]
    for t in range(T):
        y = xc[t] * w[0:1, :]
        for i in range(1, CONV_W):
            y = y + xc[t + i] * w[i:i + 1, :]
        _conv_post(y, o_ref.at[t], slice(None), j, n_q, n_k, q_scale)


def conv_sample(u, state, conv_w, n_qk_ch, q_scale):
    T, B, C = u.shape
    tc = 1024
    n_q = n_qk_ch // tc
    return pl.pallas_call(
        functools.partial(_conv_sample_kernel, T=T, n_q=n_q, n_k=n_q, q_scale=q_scale),
        grid=(C // tc,),
        in_specs=[pl.BlockSpec((T, B, tc), lambda j: (0, 0, j)),
                  pl.BlockSpec((CONV_W - 1, B, tc), lambda j: (0, 0, j)),
                  pl.BlockSpec((CONV_W, tc), lambda j: (0, j))],
        out_specs=pl.BlockSpec((T, B, tc), lambda j: (0, 0, j)),
        out_shape=jax.ShapeDtypeStruct((T, B, C), F32),
        compiler_params=_cparams(("parallel",), 16),
        name="conv_sample",
    )(u, state, conv_w)


def _gates_kernel(ba_ref, alog_ref, dtb_ref, beta_ref, gc_ref, gct_ref, *, valid, n_heads):
    b = ba_ref[:, :LANES]
    a = ba_ref[:, LANES:]
    tb = b.shape[0]
    rc = lax.broadcasted_iota(jnp.int32, (tb, LANES), 0) % CHUNK
    ok = rc < valid
    beta = jnp.where(ok, jax.nn.sigmoid(b), 0.0)
    z = a + dtb_ref[...]
    softplus = jnp.maximum(z, 0.0) + jnp.log1p(jnp.exp(-jnp.abs(z)))
    g = jnp.where(ok, -jnp.exp(alog_ref[...]) * softplus, 0.0)
    s = 1
    while s < CHUNK:
        g = g + jnp.where(rc >= s, pltpu.roll(g, s, axis=0), 0.0)
        s *= 2
    beta_ref[...] = beta
    gc_ref[...] = g
    g3 = g.reshape(tb // CHUNK, CHUNK, LANES)
    gz = jnp.concatenate([g3, jnp.zeros_like(g3)], axis=1).reshape(2 * tb, LANES)
    gct_ref[...] = gz.T[:n_heads, :]


def gdn_gates(ba, a_log, dt_bias, valid, n_heads):
    M = ba.shape[0]
    tb = min(M, 512)
    pad = lambda v: jnp.pad(v.astype(F32), (0, LANES - n_heads)).reshape(1, LANES)
    return pl.pallas_call(
        functools.partial(_gates_kernel, valid=valid, n_heads=n_heads),
        grid=(M // tb,),
        in_specs=[pl.BlockSpec((tb, 2 * LANES), lambda i: (i, 0)),
                  pl.BlockSpec((1, LANES), lambda i: (0, 0)),
                  pl.BlockSpec((1, LANES), lambda i: (0, 0))],
        out_specs=[pl.BlockSpec((tb, LANES), lambda i: (i, 0)),
                   pl.BlockSpec((tb, LANES), lambda i: (i, 0)),
                   pl.BlockSpec((n_heads, 2 * tb), lambda i: (0, i))],
        out_shape=[jax.ShapeDtypeStruct((M, LANES), F32),
                   jax.ShapeDtypeStruct((M, LANES), F32),
                   jax.ShapeDtypeStruct((n_heads, 2 * M), F32)],
        compiler_params=_cparams(("parallel",), 16),
        name="gdn_gates",
    )(ba, pad(a_log), pad(dt_bias))


def _bdot(a, b):
    return jnp.dot(a.astype(BF16), b.astype(BF16), preferred_element_type=F32)


def _bdot_nt(a, b):
    return lax.dot_general(a.astype(BF16), b.astype(BF16), (((1,), (1,)), ((), ())), preferred_element_type=F32)


def _bdot_tn(a, b):
    return lax.dot_general(a.astype(BF16), b.astype(BF16), (((0,), (0,)), ((), ())), preferred_element_type=F32)


def _gdn_kernel(q_ref, k_ref, v_ref, z_ref, beta_ref, gc_ref, gr_ref, s0_ref, ng_ref, o_ref, so_ref, s_scr,
                *, G, NC, eps):
    hg = pl.program_id(1)
    n = pl.program_id(2)
    C = CHUNK

    @pl.when(n == 0)
    def _():
        s_scr[...] = s0_ref[0]

    row = lax.broadcasted_iota(jnp.int32, (C, C), 0)
    col = lax.broadcasted_iota(jnp.int32, (C, C), 1)
    incl = row >= col
    strict = row > col
    eye = (row == col).astype(F32)
    lane = lax.broadcasted_iota(jnp.int32, (C, LANES), 1)
    beta_all = beta_ref[...]
    gc_all = gc_ref[...]
    ng = ng_ref[...]

    kk_qk = {}
    for g in range(G):
        kh = g // 2
        h = hg * G + g
        q = q_ref[:, kh * LANES:(kh + 1) * LANES]
        k = k_ref[:, kh * LANES:(kh + 1) * LANES]
        v = v_ref[:, g * LANES:(g + 1) * LANES]
        if kh not in kk_qk:
            kk_qk = {kh: (_bdot_nt(k, k), _bdot_nt(q, k))}
        kk, qk = kk_qk[kh]
        sel = lane == h
        beta = jnp.sum(jnp.where(sel, beta_all, 0.0), axis=-1, keepdims=True)
        gc = jnp.sum(jnp.where(sel, gc_all, 0.0), axis=-1, keepdims=True)
        gl = gc[C - 1:C, :]
        eg = jnp.exp(gc)
        et = jnp.exp(gl - gc)
        egl = jnp.exp(gl)
        gr = gr_ref[g:g + 1, :][:, :C]
        dec = jnp.exp(jnp.where(incl, gc - gr, -jnp.inf))
        a = beta * kk * jnp.where(strict, dec, 0.0)
        t = eye - jnp.where((row // 2 == col // 2), a, 0.0)
        s = 2
        while s < C:
            f = jnp.where(jnp.logical_and(row // (2 * s) == col // (2 * s), row // s != col // s), a, 0.0)
            t = t - _bdot(_bdot(t, f), t)
            s *= 2
        rhs = jnp.concatenate([beta * v, (beta * eg) * k], axis=1)
        sol = _bdot(t, rhs)
        u_pre = sol[:, :LANES]
        w = sol[:, LANES:]
        S = s_scr[g]
        wq = _bdot(jnp.concatenate([w, q], axis=0), S)
        u = u_pre - wq[:C]
        o = eg * wq[C:] + _bdot(qk * dec, u)
        s_scr[g] = egl * S + _bdot_tn(k * et, u)
        on = o * lax.rsqrt(jnp.mean(o * o, axis=-1, keepdims=True) + eps) * ng
        zz = z_ref[:, g * LANES:(g + 1) * LANES]
        o_ref[:, g * LANES:(g + 1) * LANES] = (on * (zz * jax.nn.sigmoid(zz))).astype(o_ref.dtype)

    @pl.when(n == NC - 1)
    def _():
        so_ref[0] = s_scr[...]


def gdn_recurrence(qkv, zsrc, z_col0, beta, gc, gct, s0, layer, norm_g, B, NC, HK, HV, G=8):
    rows = B * NC * CHUNK
    GK = G // 2
    n_g = HV // G
    kq_w = GK * LANES
    v_w = G * LANES
    k_blk0 = (HK * LANES) // kq_w
    v_blk0 = (2 * HK * LANES) // v_w
    z_blk0 = z_col0 // v_w
    rmap = lambda b, g, n: b * NC + n
    return pl.pallas_call(
        functools.partial(_gdn_kernel, G=G, NC=NC, eps=EPS),
        grid=(B, n_g, NC),
        in_specs=[
            pl.BlockSpec((CHUNK, kq_w), lambda b, g, n: (rmap(b, g, n), g)),
            pl.BlockSpec((CHUNK, kq_w), lambda b, g, n: (rmap(b, g, n), k_blk0 + g)),
            pl.BlockSpec((CHUNK, v_w), lambda b, g, n: (rmap(b, g, n), v_blk0 + g)),
            pl.BlockSpec((CHUNK, v_w), lambda b, g, n: (rmap(b, g, n), z_blk0 + g)),
            pl.BlockSpec((CHUNK, LANES), lambda b, g, n: (rmap(b, g, n), 0)),
            pl.BlockSpec((CHUNK, LANES), lambda b, g, n: (rmap(b, g, n), 0)),
            pl.BlockSpec((G, LANES), lambda b, g, n: (g, rmap(b, g, n))),
            pl.BlockSpec((None, 1, G, LANES, LANES), lambda b, g, n: (layer, b, g, 0, 0)),
            pl.BlockSpec((1, LANES), lambda b, g, n: (0, 0)),
        ],
        out_specs=[
            pl.BlockSpec((CHUNK, v_w), lambda b, g, n: (rmap(b, g, n), g)),
            pl.BlockSpec((1, G, LANES, LANES), lambda b, g, n: (b, g, 0, 0)),
        ],
        out_shape=[jax.ShapeDtypeStruct((rows, HV * LANES), BF16),
                   jax.ShapeDtypeStruct((B, HV, LANES, LANES), F32)],
        scratch_shapes=[pltpu.VMEM((G, LANES, LANES), F32)],
        compiler_params=_cparams(("parallel", "parallel", "arbitrary"), 32),
        name="gdn_recurrence",
    )(qkv, qkv, qkv, zsrc, beta, gc, gct, s0, norm_g.reshape(1, LANES))


def _rope_kernel(x_ref, cos_ref, sin_ref, qb_ref, kf_ref, kb_ref, vb_ref, *, n_q, n_k, q_scale):
    cos = cos_ref[...]
    sin = sin_ref[...]
    for g in range(n_q + n_k):
        x = x_ref[:, g * LANES:(g + 1) * LANES]
        y = x * cos + pltpu.roll(x, LANES // 2, axis=1) * sin
        if g < n_q:
            if q_scale != 1.0:
                y = y * q_scale
            qb_ref[:, g * LANES:(g + 1) * LANES] = y.astype(BF16)
        else:
            kf_ref[:, (g - n_q) * LANES:(g - n_q + 1) * LANES] = y
            kb_ref[:, (g - n_q) * LANES:(g - n_q + 1) * LANES] = y.astype(BF16)
    vb_ref[...] = x_ref[:, (n_q + n_k) * LANES:].astype(BF16)


def rope_split(qkv, cos2, sin2, n_qk_ch, n_v_ch, q_scale):
    M = qkv.shape[0]
    Lc = cos2.shape[0]
    tr = min(M, 256)
    nb = Lc // tr
    n_q = n_qk_ch // LANES
    return pl.pallas_call(
        functools.partial(_rope_kernel, n_q=n_q, n_k=n_q, q_scale=q_scale),
        grid=(M // tr,),
        in_specs=[pl.BlockSpec((tr, qkv.shape[1]), lambda i: (i, 0)),
                  pl.BlockSpec((tr, LANES), lambda i: (i % nb, 0)),
                  pl.BlockSpec((tr, LANES), lambda i: (i % nb, 0))],
        out_specs=[pl.BlockSpec((tr, n_qk_ch), lambda i: (i, 0)),
                   pl.BlockSpec((tr, n_qk_ch), lambda i: (i, 0)),
                   pl.BlockSpec((tr, n_qk_ch), lambda i: (i, 0)),
                   pl.BlockSpec((tr, n_v_ch), lambda i: (i, 0))],
        out_shape=[jax.ShapeDtypeStruct((M, n_qk_ch), BF16),
                   jax.ShapeDtypeStruct((M, n_qk_ch), F32),
                   jax.ShapeDtypeStruct((M, n_qk_ch), BF16),
                   jax.ShapeDtypeStruct((M, n_v_ch), BF16)],
        compiler_params=_cparams(("parallel",), 40),
        name="rope_split",
    )(qkv, cos2, sin2)


def _rope_tables(pos):
    half = LANES // 2
    inv = 1.0 / (ROPE_THETA ** (jnp.arange(half, dtype=F32) * (2.0 / LANES)))
    ang = pos.astype(F32)[:, None] * inv[None, :]
    cos, sin = jnp.cos(ang), jnp.sin(ang)
    return jnp.concatenate([cos, cos], axis=-1), jnp.concatenate([-sin, sin], axis=-1)


def _diff_lambda_in_kernel(lam_ref, lam_init):
    lf = lam_ref[...]
    s1 = jnp.sum(lf[0:1, :] * lf[1:2, :], axis=-1, keepdims=True)
    s2 = jnp.sum(lf[2:3, :] * lf[3:4, :], axis=-1, keepdims=True)
    return jnp.exp(s1) - jnp.exp(s2) + lam_init


def _subln(o, g_ref, lam_init, eps=1e-5):
    y = o * lax.rsqrt(jnp.mean(o * o, axis=-1, keepdims=True) + eps)
    return (y * g_ref[...]) * (1.0 - lam_init)


def _flash_kernel(q_ref, k_ref, v_ref, lam_ref, g_ref, o_ref, m_scr, l_scr, acc_scr, *, T, scale, lam_init):
    qi = pl.program_id(2)
    DH = LANES
    m_scr[...] = jnp.full(m_scr.shape, -jnp.inf, F32)
    l_scr[...] = jnp.zeros(l_scr.shape, F32)
    acc_scr[...] = jnp.zeros(acc_scr.shape, F32)
    q = q_ref[...]

    def tile(kt, masked):
        r0 = pl.multiple_of(kt * T, T)
        k = k_ref[pl.ds(r0, T), :]
        v = v_ref[pl.ds(r0, T), :]
        for c in range(2):
            s = lax.dot_general(q[:, c * DH:(c + 1) * DH], k[:, c * DH:(c + 1) * DH],
                                (((1,), (1,)), ((), ())), preferred_element_type=F32) * scale
            if masked:
                rr = lax.broadcasted_iota(jnp.int32, (T, T), 0)
                cc = lax.broadcasted_iota(jnp.int32, (T, T), 1)
                s = jnp.where(cc <= rr, s, -jnp.inf)
            m_old = m_scr[c]
            m_new = jnp.maximum(m_old, jnp.max(s, axis=-1, keepdims=True))
            corr = jnp.exp(m_old - m_new)
            e = jnp.exp(s - m_new)
            l_scr[c] = l_scr[c] * corr + jnp.sum(e, axis=-1, keepdims=True)
            acc_scr[c] = acc_scr[c] * corr + jnp.dot(e.astype(BF16), v, preferred_element_type=F32)
            m_scr[c] = m_new

    def body(kt, carry):
        tile(kt, False)
        return carry

    lax.fori_loop(0, qi, body, 0)
    tile(qi, True)
    lam = _diff_lambda_in_kernel(lam_ref, lam_init)
    o = acc_scr[0] / l_scr[0] - lam * (acc_scr[1] / l_scr[1])
    o_ref[...] = _subln(o, g_ref, lam_init).astype(o_ref.dtype)


def flash_diff_attention(qb, kb, vb, lam_rows, subln, B, L, H, lam_init):
    T = 512
    W = 2 * LANES
    nq = L // T
    return pl.pallas_call(
        functools.partial(_flash_kernel, T=T, scale=LANES ** -0.5, lam_init=lam_init),
        grid=(B, H, nq),
        in_specs=[pl.BlockSpec((T, W), lambda b, h, i: (b * nq + i, h)),
                  pl.BlockSpec((L, W), lambda b, h, i: (b, h)),
                  pl.BlockSpec((L, W), lambda b, h, i: (b, h)),
                  pl.BlockSpec((4, LANES), lambda b, h, i: (0, 0)),
                  pl.BlockSpec((1, W), lambda b, h, i: (0, 0))],
        out_specs=pl.BlockSpec((T, W), lambda b, h, i: (b * nq + i, h)),
        out_shape=jax.ShapeDtypeStruct((B * L, H * W), BF16),
        scratch_shapes=[pltpu.VMEM((2, T, 1), F32), pltpu.VMEM((2, T, 1), F32), pltpu.VMEM((2, T, W), F32)],
        compiler_params=_cparams(("parallel", "parallel", "arbitrary"), 40),
        name="flash_diff_attention",
    )(qb, kb, vb, lam_rows, subln.reshape(1, W))


def _paged_kernel(pt_ref, q_ref, kn_ref, vn_ref, lam_ref, g_ref, *rest, P, H, T, TP, n_steps, lam_init):
    k_refs = rest[:P]
    v_refs = rest[P:3 * P]
    o_ref = rest[3 * P]
    m_scr, l_scr, acc_scr = rest[3 * P + 1:]
    p = pl.program_id(1)
    HC = 2 * H
    R = SLOT_ROWS
    W = 2 * LANES
    q = q_ref[0]

    def update(s, v_of_head):
        m_old = m_scr[...]
        m_new = jnp.maximum(m_old, jnp.max(s, axis=-1, keepdims=True))
        corr = jnp.exp(m_old - m_new)
        e = jnp.exp(s - m_new)
        l_scr[...] = l_scr[...] * corr + jnp.sum(e, axis=-1, keepdims=True)
        m_scr[...] = m_new
        eb = e.astype(BF16)
        for h in range(H):
            rs = slice(h * 2 * R, (h + 1) * 2 * R)
            acc_scr[rs, :] = acc_scr[rs, :] * corr[rs, :] + jnp.dot(eb[rs, :], v_of_head(h),
                                                                    preferred_element_type=F32)

    @pl.when(p == 0)
    def _():
        m_scr[...] = jnp.full(m_scr.shape, -jnp.inf, F32)
        l_scr[...] = jnp.zeros(l_scr.shape, F32)
        acc_scr[...] = jnp.zeros(acc_scr.shape, F32)
        kn = kn_ref[0]
        vn = vn_ref[0]
        ss = []
        for hc in range(HC):
            ss.append(lax.dot_general(q[hc * R:(hc + 1) * R, :], kn[hc * R:(hc + 1) * R, :],
                                      (((1,), (1,)), ((), ())), preferred_element_type=F32))
        s = jnp.concatenate(ss, axis=0)
        qr = lax.broadcasted_iota(jnp.int32, (HC * R, R), 0) % R
        kc = lax.broadcasted_iota(jnp.int32, (HC * R, R), 1)
        s = jnp.where(jnp.logical_and(kc <= qr, kc < T), s, -jnp.inf)
        update(s, lambda h: vn[h * R:(h + 1) * R, :])

    for r in range(P):
        kr = k_refs[r]
        vlo, vhi = v_refs[2 * r], v_refs[2 * r + 1]
        ss = []
        for hc in range(HC):
            k_hc = kr[pl.ds(hc, TP, stride=HC), :].astype(BF16)
            ss.append(lax.dot_general(q[hc * R:(hc + 1) * R, :], k_hc,
                                      (((1,), (1,)), ((), ())), preferred_element_type=F32))
        s = jnp.concatenate(ss, axis=0)
        update(s, lambda h: jnp.concatenate([vlo[pl.ds(h, TP, stride=H), :], vhi[pl.ds(h, TP, stride=H), :]],
                                            axis=1).astype(BF16))

    @pl.when(p == n_steps - 1)
    def _():
        lam = _diff_lambda_in_kernel(lam_ref, lam_init)
        po = acc_scr[...] / l_scr[...]
        for h in range(H):
            o = po[h * 2 * R:h * 2 * R + R, :] - lam * po[h * 2 * R + R:(h + 1) * 2 * R, :]
            o_ref[0, :, h * W:(h + 1) * W] = _subln(o, g_ref, lam_init).astype(o_ref.dtype)


def paged_diff_attention(qs, kn, vn, k_pool, v_pool, layer, page_table, lam_rows, subln, H, T, lam_init):
    B = qs.shape[0]
    n_pages = page_table.shape[1]
    TP = k_pool.shape[2] // (2 * H)
    P = 4
    n_steps = n_pages // P
    W = 2 * LANES
    R = SLOT_ROWS

    def kmap(r):
        return lambda b, p, pt: (layer, pt[b, p * P + r], 0, 0)

    def vmap(r, half):
        return lambda b, p, pt: (layer, pt[b, p * P + r], 0, half)

    in_specs = [pl.BlockSpec((1, 2 * H * R, LANES), lambda b, p, pt: (b, 0, 0)),
                pl.BlockSpec((1, 2 * H * R, LANES), lambda b, p, pt: (b, 0, 0)),
                pl.BlockSpec((1, H * R, W), lambda b, p, pt: (b, 0, 0)),
                pl.BlockSpec((4, LANES), lambda b, p, pt: (0, 0)),
                pl.BlockSpec((1, W), lambda b, p, pt: (0, 0))]
    in_specs += [pl.BlockSpec((None, None, TP * 2 * H, LANES), kmap(r)) for r in range(P)]
    for r in range(P):
        in_specs += [pl.BlockSpec((None, None, TP * H, LANES), vmap(r, half)) for half in range(2)]
    return pl.pallas_call(
        functools.partial(_paged_kernel, P=P, H=H, T=T, TP=TP, n_steps=n_steps, lam_init=lam_init),
        grid_spec=pltpu.PrefetchScalarGridSpec(
            num_scalar_prefetch=1,
            grid=(B, n_steps),
            in_specs=in_specs,
            out_specs=pl.BlockSpec((1, R, H * W), lambda b, p, pt: (b, 0, 0)),
            scratch_shapes=[pltpu.VMEM((2 * H * R, 1), F32), pltpu.VMEM((2 * H * R, 1), F32),
                            pltpu.VMEM((2 * H * R, W), F32)],
        ),
        out_shape=jax.ShapeDtypeStruct((B, R, H * W), BF16),
        compiler_params=_cparams(("parallel", "arbitrary"), 40),
        name="paged_diff_attention",
    )(page_table, qs, kn, vn, lam_rows, subln.reshape(1, W), *([k_pool] * P), *([v_pool] * (2 * P)))


def _mem_attn_kernel(q_ref, k_ref, v_ref, o_ref, *, H, scale):
    q = q_ref[0]
    k = k_ref[0].astype(BF16)
    v = v_ref[0].astype(BF16)
    for h in range(H):
        sl = slice(h * LANES, (h + 1) * LANES)
        s = lax.dot_general(q[:, sl], k[:, sl], (((1,), (1,)), ((), ())), preferred_element_type=F32) * scale
        m = jnp.max(s, axis=-1, keepdims=True)
        e = jnp.exp(s - m)
        p = e / jnp.sum(e, axis=-1, keepdims=True)
        o_ref[0, :, sl] = jnp.dot(p.astype(BF16), v[:, sl], preferred_element_type=F32).astype(o_ref.dtype)


def mem_attention(q, mk, mv, H):
    B, L, D = q.shape
    Mm = mk.shape[1]
    tq = min(L, 512)
    return pl.pallas_call(
        functools.partial(_mem_attn_kernel, H=H, scale=LANES ** -0.5),
        grid=(B, L // tq),
        in_specs=[pl.BlockSpec((1, tq, D), lambda b, i: (b, i, 0)),
                  pl.BlockSpec((1, Mm, D), lambda b, i: (b, 0, 0)),
                  pl.BlockSpec((1, Mm, D), lambda b, i: (b, 0, 0))],
        out_specs=pl.BlockSpec((1, tq, D), lambda b, i: (b, i, 0)),
        out_shape=jax.ShapeDtypeStruct((B, L, D), BF16),
        compiler_params=_cparams(("parallel", "parallel"), 24),
        name="mem_attention",
    )(q, mk, mv)


def _gdn_mixer(hn, B, L, conv_state, s0, j, w_in, w_ba, conv_w, a_log, dt_bias, norm_g, w_out, x_res, dims):
    HK, HV, n_qk, n_conv = dims
    h = matmul(hn, w_in, j, n_out=n_conv + HV * LANES)
    ba = matmul(hn, w_ba, j)
    q_scale = LANES ** -0.5
    if conv_state is None:
        qkv = conv_prompt(h, conv_w, B, L, n_qk, n_conv, q_scale)
        new_conv = h.reshape(B, L, -1)[:, L - (CONV_W - 1):, :n_conv]
        NC = L // CHUNK
        zsrc, z_col0 = h, n_conv
        beta, gc, gct = gdn_gates(ba, a_log, dt_bias, CHUNK, HV)
    else:
        u = h[:, :n_conv].reshape(B, L, n_conv)
        ut = jnp.swapaxes(u, 0, 1)
        st = jnp.swapaxes(conv_state, 0, 1)
        y = conv_sample(ut, st, conv_w, n_qk, q_scale)
        new_conv = jnp.concatenate([conv_state, u], axis=1)[:, L:, :]
        padrows = lambda t: jnp.pad(t.reshape(B, L, -1), ((0, 0), (0, CHUNK - L), (0, 0))).reshape(B * CHUNK, -1)
        qkv = padrows(jnp.swapaxes(y, 0, 1))
        zsrc, z_col0 = padrows(h[:, n_conv:]), 0
        NC = 1
        beta, gc, gct = gdn_gates(padrows(ba), a_log, dt_bias, L, HV)
    og, s_new = gdn_recurrence(qkv, zsrc, z_col0, beta, gc, gct, s0, j if s0.shape[0] > 1 else 0, norm_g,
                               B, NC, HK, HV)
    if conv_state is not None:
        og = og.reshape(B, CHUNK, -1)[:, :L].reshape(B * L, -1)
    return matmul(og, w_out, j, res=x_res), new_conv, s_new


def _forward(x, mem_kv_fn, B, L, pos, gdn_state, attn_fn, attn_q_scale, params):
    (norm_mix, gdn_w_in, gdn_w_ba, gdn_conv_w, gdn_a_log, gdn_dt_bias, gdn_norm, gdn_w_out, attn_w_qkv,
     attn_lambda, attn_subln, attn_w_o, norm_xattn, w_xq, w_xo, norm_ffn, w_up, w_down, norm_final) = params
    depth = norm_mix.shape[0]
    D = x.shape[1]
    HV = gdn_a_log.shape[1]
    HK = HV // 2
    n_qk = HK * LANES
    n_conv = gdn_conv_w.shape[2]
    XH = w_xq.shape[2] // LANES
    cos2, sin2 = _rope_tables(pos)
    if L < 256:
        cos2, sin2 = jnp.tile(cos2, (B, 1)), jnp.tile(sin2, (B, 1))
    Lq = max(L, SLOT_ROWS)
    convs, deltas, ks, vs = [], [], [], []
    for i in range(depth):
        j = i // 2
        hn = rmsnorm(x, norm_mix[i], BF16)
        if i % 2 == 0:
            conv_state, s0 = gdn_state(j)
            x, new_conv, s_new = _gdn_mixer(hn, B, L, conv_state, s0, j, gdn_w_in, gdn_w_ba, gdn_conv_w[j],
                                            gdn_a_log[j], gdn_dt_bias[j], gdn_norm[j], gdn_w_out, x,
                                            (HK, HV, n_qk, n_conv))
            convs.append(new_conv)
            deltas.append(s_new)
        else:
            lam_init = 0.8 - 0.6 * math.exp(-0.3 * i)
            n_qk_a = attn_w_qkv.shape[2] // 3
            qkv = matmul(hn, attn_w_qkv, j)
            qb, kf, kb, vb = rope_split(qkv, cos2, sin2, n_qk_a, n_qk_a, attn_q_scale)
            oa = attn_fn(j, qb, kb, vb, attn_lambda[j], attn_subln[j], lam_init)
            x = matmul(oa, attn_w_o, j, res=x)
            ks.append(kf)
            vs.append(qkv[:, 2 * n_qk_a:])
        hx = rmsnorm(x, norm_xattn[i], BF16)
        q = matmul(hx, w_xq, i, out_dtype=BF16).reshape(B, L, -1)
        mk, mv = mem_kv_fn(i)
        om = mem_attention(jnp.pad(q, ((0, 0), (0, Lq - L), (0, 0))), mk, mv, XH)[:, :L]
        x = matmul(om.reshape(B * L, -1), w_xo, i, res=x)
        hf = rmsnorm(x, norm_ffn[i], BF16)
        up = matmul(hf, w_up, i, out_dtype=BF16, act="relu2")
        x = matmul(up, w_down, i, res=x)
    y = rmsnorm(x, norm_final, F32)
    return y, convs, deltas, ks, vs


def kernel(x_prompt, x_sample, mem_prompt, state_conv, state_delta, cache_k, cache_v, cache_mem_k, cache_mem_v, page_table, norm_mix, gdn_w_in, gdn_conv_w, gdn_a_log, gdn_dt_bias, gdn_norm, gdn_w_out, attn_w_qkv, attn_lambda, attn_subln, attn_w_o, norm_xattn, norm_mem, w_xq, w_xkv, w_xo, norm_ffn, w_up, w_down, norm_final):
    Bp, Lp, D = x_prompt.shape
    Bs, Ls, _ = x_sample.shape
    depth = norm_mix.shape[0]
    HV = gdn_a_log.shape[1]
    n_conv = gdn_conv_w.shape[2]
    n_attn, n_pool, TP, H = cache_k.shape[:4]
    W = 2 * LANES
    XH = w_xq.shape[2] // LANES
    Mm = mem_prompt.shape[1]
    past_len = page_table.shape[1] * TP

    n_main = n_conv + HV * LANES
    padc = lambda w: jnp.pad(w, ((0, 0), (0, 0), (0, LANES - HV)))
    gdn_w_ba = jnp.concatenate([padc(gdn_w_in[:, :, n_main:n_main + HV]), padc(gdn_w_in[:, :, n_main + HV:])], axis=-1)

    params = (norm_mix, gdn_w_in, gdn_w_ba, gdn_conv_w, gdn_a_log, gdn_dt_bias, gdn_norm, gdn_w_out, attn_w_qkv,
              attn_lambda, attn_subln, attn_w_o, norm_xattn, w_xq, w_xo, norm_ffn, w_up, w_down, norm_final)

    p_mk, p_mv = [], []

    def prompt_mem(i):
        mn = rmsnorm(mem_prompt.reshape(Bp * Mm, D), norm_mem[i], BF16)
        kv = matmul(mn, w_xkv, i).reshape(Bp, Mm, 2 * XH * LANES)
        mk, mv = kv[:, :, :XH * LANES], kv[:, :, XH * LANES:]
        p_mk.append(mk.reshape(Bp, Mm, XH, LANES))
        p_mv.append(mv.reshape(Bp, Mm, XH, LANES))
        return mk, mv

    def prompt_attn(j, qb, kb, vb, lam_rows, subln, lam_init):
        return flash_diff_attention(qb, kb, vb, lam_rows, subln, Bp, Lp, H, lam_init)

    zero_state = jnp.zeros((1, Bp, HV, LANES, LANES), F32)
    yp, p_conv, p_delta, p_k, p_v = _forward(
        x_prompt.reshape(Bp * Lp, D), prompt_mem, Bp, Lp, jnp.arange(Lp), lambda j: (None, zero_state),
        prompt_attn, 1.0, params)

    k_pool = cache_k.reshape(n_attn, n_pool, TP * H * 2, LANES)
    v_pool = cache_v.reshape(n_attn, n_pool, TP * H, W)
    R = SLOT_ROWS

    def sample_attn(j, qb, kb, vb, lam_rows, subln, lam_init):
        def slots(t, n_slot):
            t = jnp.swapaxes(t.reshape(Bs, Ls, n_slot, -1), 1, 2)
            t = jnp.pad(t, ((0, 0), (0, 0), (0, R - Ls), (0, 0)))
            return t.reshape(Bs, n_slot * R, t.shape[-1])

        oa = paged_diff_attention(slots(qb, 2 * H), slots(kb, 2 * H), slots(vb, H), k_pool, v_pool, j, page_table,
                                  lam_rows, subln, H, Ls, lam_init)
        return oa[:, :Ls, :].reshape(Bs * Ls, H * W)

    def sample_gdn_state(j):
        return state_conv[j], state_delta

    ys, s_conv, s_delta, s_k, s_v = _forward(
        x_sample.reshape(Bs * Ls, D), lambda i: (cache_mem_k[i].reshape(Bs, Mm, -1), cache_mem_v[i].reshape(Bs, Mm, -1)),
        Bs, Ls, past_len + jnp.arange(Ls), sample_gdn_state, sample_attn, LANES ** -0.5, params)

    return (yp.reshape(Bp, Lp, D), ys.reshape(Bs, Ls, D),
            jnp.stack(p_conv), jnp.stack(p_delta), jnp.stack(s_conv), jnp.stack(s_delta),
            jnp.stack(p_k).reshape(n_attn, Bp, Lp, H, 2, LANES), jnp.stack(p_v).reshape(n_attn, Bp, Lp, H, W),
            jnp.stack(s_k).reshape(n_attn, Bs, Ls, H, 2, LANES), jnp.stack(s_v).reshape(n_attn, Bs, Ls, H, W),
            jnp.stack(p_mk), jnp.stack(p_mv))
```

```python
import functools
import math

import jax
import jax.numpy as jnp
from jax import lax
from jax.experimental import pallas as pl
from jax.experimental.pallas import tpu as pltpu

F32 = jnp.float32
BF16 = jnp.bfloat16

LANES = 128
CHUNK = 64
CONV_W = 4
EPS = 1e-6
ROPE_THETA = 10000.0
VMEM_CAP = 60 * 1024 * 1024
SLOT_ROWS = 16


def _cparams(sem, vmem_mb):
    return pltpu.CompilerParams(dimension_semantics=sem, vmem_limit_bytes=min(VMEM_CAP, vmem_mb * 1024 * 1024))


def _rmsnorm_kernel(x_ref, g_ref, o_ref, *, eps):
    x = x_ref[...]
    y = x * lax.rsqrt(jnp.mean(x * x, axis=-1, keepdims=True) + eps)
    o_ref[...] = (y * g_ref[...]).astype(o_ref.dtype)


def rmsnorm(x, gain, out_dtype, eps=EPS):
    M, D = x.shape
    tm = min(M, 512)
    return pl.pallas_call(
        functools.partial(_rmsnorm_kernel, eps=eps),
        grid=(M // tm,),
        in_specs=[pl.BlockSpec((tm, D), lambda i: (i, 0)), pl.BlockSpec((1, D), lambda i: (0, 0))],
        out_specs=pl.BlockSpec((tm, D), lambda i: (i, 0)),
        out_shape=jax.ShapeDtypeStruct((M, D), out_dtype),
        compiler_params=_cparams(("parallel",), 32),
        name="rmsnorm",
    )(x, gain.reshape(1, D))


def _mm_kernel(*refs, nk, act, has_res):
    if has_res:
        a_ref, w_ref, r_ref, o_ref = refs[:4]
        scratch = refs[4:]
    else:
        a_ref, w_ref, o_ref = refs[:3]
        r_ref = None
        scratch = refs[3:]
    p = jnp.dot(a_ref[...].astype(BF16), w_ref[...].astype(BF16), preferred_element_type=F32)

    def finish(acc):
        if act == "relu2":
            r = jnp.maximum(acc, 0.0)
            acc = r * r
        if has_res:
            acc = acc + r_ref[...]
        o_ref[...] = acc.astype(o_ref.dtype)

    if nk == 1:
        finish(p)
    else:
        acc_ref = scratch[0]
        k = pl.program_id(2)

        @pl.when(k == 0)
        def _():
            acc_ref[...] = p

        @pl.when(k > 0)
        def _():
            acc_ref[...] += p

        @pl.when(k == nk - 1)
        def _():
            finish(acc_ref[...])


def matmul(a, w, layer, *, n_out=None, out_dtype=F32, act=None, res=None):
    M, K = a.shape
    N = n_out if n_out is not None else w.shape[2]
    if M >= 2048 and K <= 2048:
        tm, tn, tk = 2048, min(N, 512), K
    elif M >= 2048 and K <= 4096:
        tm, tn, tk = 1024, min(N, 512), K
    elif M >= 2048:
        tm, tn, tk = 2048, min(N, 512), 1024
    else:
        tm, tn, tk = M, min(N, 1024), min(K, 2048)
    nk = K // tk
    in_specs = [pl.BlockSpec((tm, tk), lambda i, j, k: (i, k)),
                pl.BlockSpec((None, tk, tn), lambda i, j, k: (layer, k, j))]
    args = [a, w]
    if res is not None:
        in_specs.append(pl.BlockSpec((tm, tn), lambda i, j, k: (i, j)))
        args.append(res)
    out_b = jnp.dtype(out_dtype).itemsize
    est = 2 * (tm * tk * a.dtype.itemsize + tk * tn * 4 + tm * tn * out_b + (tm * tn * 4 if res is not None else 0))
    est += tm * tn * 4 * (2 if nk > 1 else 1) + tk * tn * 2
    return pl.pallas_call(
        functools.partial(_mm_kernel, nk=nk, act=act, has_res=res is not None),
        grid=(M // tm, N // tn, nk),
        in_specs=in_specs,
        out_specs=pl.BlockSpec((tm, tn), lambda i, j, k: (i, j)),
        out_shape=jax.ShapeDtypeStruct((M, N), out_dtype),
        scratch_shapes=[pltpu.VMEM((tm, tn), F32)] if nk > 1 else [],
        compiler_params=_cparams(("parallel", "parallel", "arbitrary"), est // (1024 * 1024) + 8),
        name="matmul",
    )(*args)


def _l2_groups(y, scale):
    outs = []
    for g in range(y.shape[-1] // LANES):
        yg = y[:, g * LANES:(g + 1) * LANES]
        ss = jnp.sum(yg * yg, axis=-1, keepdims=True)
        outs.append(yg * (lax.rsqrt(ss + 1e-6) * scale))
    return outs


def _conv_post(y, o_ref, rows, j, n_q, n_k, q_scale):
    y = y * jax.nn.sigmoid(y)

    @pl.when(j < n_q)
    def _():
        for g, yg in enumerate(_l2_groups(y, q_scale)):
            o_ref[rows, g * LANES:(g + 1) * LANES] = yg

    @pl.when(jnp.logical_and(j >= n_q, j < n_q + n_k))
    def _():
        for g, yg in enumerate(_l2_groups(y, 1.0)):
            o_ref[rows, g * LANES:(g + 1) * LANES] = yg

    @pl.when(j >= n_q + n_k)
    def _():
        o_ref[rows, :] = y


def _conv_prompt_kernel(x_ref, w_ref, o_ref, *, L, TR, n_q, n_k, q_scale):
    j = pl.program_id(1)
    w = w_ref[...]

    def body(i, carry):
        r0 = pl.multiple_of(i * TR, TR)
        xa = x_ref[pl.ds(r0, TR), :]
        pstart = pl.multiple_of(jnp.maximum(r0 - 8, 0), 8)
        xp = jnp.where(i == 0, 0.0, x_ref[pl.ds(pstart, 8), :])
        xx = jnp.concatenate([xp, xa], axis=0)
        y = xa * w[3:4, :]
        for s in range(1, CONV_W):
            y = y + pltpu.roll(xx, s, axis=0)[8:, :] * w[3 - s:4 - s, :]
        _conv_post(y, o_ref, pl.ds(r0, TR), j, n_q, n_k, q_scale)
        return carry

    lax.fori_loop(0, L // TR, body, 0)


def conv_prompt(h, conv_w, B, L, n_qk_ch, n_ch, q_scale):
    tc = 256
    n_q = n_qk_ch // tc
    return pl.pallas_call(
        functools.partial(_conv_prompt_kernel, L=L, TR=256, n_q=n_q, n_k=n_q, q_scale=q_scale),
        grid=(B, n_ch // tc),
        in_specs=[pl.BlockSpec((L, tc), lambda b, j: (b, j)), pl.BlockSpec((CONV_W, tc), lambda b, j: (0, j))],
        out_specs=pl.BlockSpec((L, tc), lambda b, j: (b, j)),
        out_shape=jax.ShapeDtypeStruct((B * L, n_ch), F32),
        compiler_params=_cparams(("parallel", "parallel"), 40),
        name="conv_prompt",
    )(h, conv_w)


def _conv_sample_kernel(u_ref, st_ref, w_ref, o_ref, *, T, n_q, n_k, q_scale):
    j = pl.program_id(0)
    w = w_ref[...]
    xc = [st_ref[i] for i in range(CONV_W - 1)] + [u_ref[t] for t in range(T)]
    for t in range(T):
        y = xc[t] * w[0:1, :]
        for i in range(1, CONV_W):
            y = y + xc[t + i] * w[i:i + 1, :]
        _conv_post(y, o_ref.at[t], slice(None), j, n_q, n_k, q_scale)


def conv_sample(u, state, conv_w, n_qk_ch, q_scale):
    T, B, C = u.shape
    tc = 1024
    n_q = n_qk_ch // tc
    return pl.pallas_call(
        functools.partial(_conv_sample_kernel, T=T, n_q=n_q, n_k=n_q, q_scale=q_scale),
        grid=(C // tc,),
        in_specs=[pl.BlockSpec((T, B, tc), lambda j: (0, 0, j)),
                  pl.BlockSpec((CONV_W - 1, B, tc), lambda j: (0, 0, j)),
                  pl.BlockSpec((CONV_W, tc), lambda j: (0, j))],
        out_specs=pl.BlockSpec((T, B, tc), lambda j: (0, 0, j)),
        out_shape=jax.ShapeDtypeStruct((T, B, C), F32),
        compiler_params=_cparams(("parallel",), 16),
        name="conv_sample",
    )(u, state, conv_w)


def _gates_kernel(ba_ref, alog_ref, dtb_ref, beta_ref, gc_ref, gct_ref, *, valid, n_heads):
    b = ba_ref[:, :LANES]
    a = ba_ref[:, LANES:]
    tb = b.shape[0]
    rc = lax.broadcasted_iota(jnp.int32, (tb, LANES), 0) % CHUNK
    ok = rc < valid
    beta = jnp.where(ok, jax.nn.sigmoid(b), 0.0)
    z = a + dtb_ref[...]
    softplus = jnp.maximum(z, 0.0) + jnp.log1p(jnp.exp(-jnp.abs(z)))
    g = jnp.where(ok, -jnp.exp(alog_ref[...]) * softplus, 0.0)
    s = 1
    while s < CHUNK:
        g = g + jnp.where(rc >= s, pltpu.roll(g, s, axis=0), 0.0)
        s *= 2
    beta_ref[...] = beta
    gc_ref[...] = g
    g3 = g.reshape(tb // CHUNK, CHUNK, LANES)
    gz = jnp.concatenate([g3, jnp.zeros_like(g3)], axis=1).reshape(2 * tb, LANES)
    gct_ref[...] = gz.T[:n_heads, :]


def gdn_gates(ba, a_log, dt_bias, valid, n_heads):
    M = ba.shape[0]
    tb = min(M, 512)
    pad = lambda v: jnp.pad(v.astype(F32), (0, LANES - n_heads)).reshape(1, LANES)
    return pl.pallas_call(
        functools.partial(_gates_kernel, valid=valid, n_heads=n_heads),
        grid=(M // tb,),
        in_specs=[pl.BlockSpec((tb, 2 * LANES), lambda i: (i, 0)),
                  pl.BlockSpec((1, LANES), lambda i: (0, 0)),
                  pl.BlockSpec((1, LANES), lambda i: (0, 0))],
        out_specs=[pl.BlockSpec((tb, LANES), lambda i: (i, 0)),
                   pl.BlockSpec((tb, LANES), lambda i: (i, 0)),
                   pl.BlockSpec((n_heads, 2 * tb), lambda i: (0, i))],
        out_shape=[jax.ShapeDtypeStruct((M, LANES), F32),
                   jax.ShapeDtypeStruct((M, LANES), F32),
                   jax.ShapeDtypeStruct((n_heads, 2 * M), F32)],
        compiler_params=_cparams(("parallel",), 16),
        name="gdn_gates",
    )(ba, pad(a_log), pad(dt_bias))


def _bdot(a, b):
    return jnp.dot(a.astype(BF16), b.astype(BF16), preferred_element_type=F32)


def _bdot_nt(a, b):
    return lax.dot_general(a.astype(BF16), b.astype(BF16), (((1,), (1,)), ((), ())), preferred_element_type=F32)


def _bdot_tn(a, b):
    return lax.dot_general(a.astype(BF16), b.astype(BF16), (((0,), (0,)), ((), ())), preferred_element_type=F32)


def _gdn_kernel(q_ref, k_ref, v_ref, z_ref, beta_ref, gc_ref, gr_ref, s0_ref, ng_ref, o_ref, so_ref, s_scr,
                *, G, NC, eps):
    hg = pl.program_id(1)
    n = pl.program_id(2)
    C = CHUNK

    @pl.when(n == 0)
    def _():
        s_scr[...] = s0_ref[0]

    row = lax.broadcasted_iota(jnp.int32, (C, C), 0)
    col = lax.broadcasted_iota(jnp.int32, (C, C), 1)
    incl = row >= col
    strict = row > col
    eye = (row == col).astype(F32)
    lane = lax.broadcasted_iota(jnp.int32, (C, LANES), 1)
    beta_all = beta_ref[...]
    gc_all = gc_ref[...]
    ng = ng_ref[...]
    heads = range(G)
    dot = functools.partial(jnp.dot, preferred_element_type=F32)

    qs = [q_ref[:, kh * LANES:(kh + 1) * LANES] for kh in range(G // 2)]
    ks = [k_ref[:, kh * LANES:(kh + 1) * LANES] for kh in range(G // 2)]
    qkk = [_bdot_nt(jnp.concatenate([qs[kh], ks[kh]], axis=0), ks[kh]) for kh in range(G // 2)]

    beta, eg, et, egl, dec, a, t = [], [], [], [], [], [], []
    for g in heads:
        sel = lane == hg * G + g
        b_g = jnp.sum(jnp.where(sel, beta_all, 0.0), axis=-1, keepdims=True)
        gc = jnp.sum(jnp.where(sel, gc_all, 0.0), axis=-1, keepdims=True)
        gl = gc[C - 1:C, :]
        gr = gr_ref[g:g + 1, :][:, :C]
        d_g = jnp.exp(jnp.where(incl, gc - gr, -jnp.inf))
        a_g = b_g * qkk[g // 2][C:] * jnp.where(strict, d_g, 0.0)
        beta.append(b_g)
        eg.append(jnp.exp(gc))
        et.append(jnp.exp(gl - gc))
        egl.append(jnp.exp(gl))
        dec.append(d_g)
        a.append(a_g)
        t.append(eye - jnp.where(row // 2 == col // 2, a_g, 0.0))

    s = 2
    while s < C:
        fmask = jnp.logical_and(row // (2 * s) == col // (2 * s), row // s != col // s)
        tb = [t[g].astype(BF16) for g in heads]
        x = [dot(tb[g], jnp.where(fmask, a[g], 0.0).astype(BF16)) for g in heads]
        t = [t[g] - dot(x[g].astype(BF16), tb[g]) for g in heads]
        s *= 2

    sol = []
    for g in heads:
        v = v_ref[:, g * LANES:(g + 1) * LANES]
        rhs = jnp.concatenate([beta[g] * v, (beta[g] * eg[g]) * ks[g // 2]], axis=1)
        sol.append(_bdot(t[g], rhs))
    S = [s_scr[g] for g in heads]
    wq = [_bdot(jnp.concatenate([sol[g][:, LANES:], qs[g // 2]], axis=0), S[g]) for g in heads]
    u = [sol[g][:, :LANES] - wq[g][:C] for g in heads]
    o1 = [_bdot(qkk[g // 2][:C] * dec[g], u[g]) for g in heads]
    for g in heads:
        s_scr[g] = egl[g] * S[g] + _bdot_tn(ks[g // 2] * et[g], u[g])
    for g in heads:
        o = eg[g] * wq[g][C:] + o1[g]
        on = o * lax.rsqrt(jnp.mean(o * o, axis=-1, keepdims=True) + eps) * ng
        zz = z_ref[:, g * LANES:(g + 1) * LANES]
        o_ref[:, g * LANES:(g + 1) * LANES] = (on * (zz * jax.nn.sigmoid(zz))).astype(o_ref.dtype)

    @pl.when(n == NC - 1)
    def _():
        so_ref[0] = s_scr[...]


def gdn_recurrence(qkv, zsrc, z_col0, beta, gc, gct, s0, layer, norm_g, B, NC, HK, HV, G=32):
    rows = B * NC * CHUNK
    GK = G // 2
    n_g = HV // G
    kq_w = GK * LANES
    v_w = G * LANES
    k_blk0 = (HK * LANES) // kq_w
    v_blk0 = (2 * HK * LANES) // v_w
    z_blk0 = z_col0 // v_w
    rmap = lambda b, g, n: b * NC + n
    return pl.pallas_call(
        functools.partial(_gdn_kernel, G=G, NC=NC, eps=EPS),
        grid=(B, n_g, NC),
        in_specs=[
            pl.BlockSpec((CHUNK, kq_w), lambda b, g, n: (rmap(b, g, n), g)),
            pl.BlockSpec((CHUNK, kq_w), lambda b, g, n: (rmap(b, g, n), k_blk0 + g)),
            pl.BlockSpec((CHUNK, v_w), lambda b, g, n: (rmap(b, g, n), v_blk0 + g)),
            pl.BlockSpec((CHUNK, v_w), lambda b, g, n: (rmap(b, g, n), z_blk0 + g)),
            pl.BlockSpec((CHUNK, LANES), lambda b, g, n: (rmap(b, g, n), 0)),
            pl.BlockSpec((CHUNK, LANES), lambda b, g, n: (rmap(b, g, n), 0)),
            pl.BlockSpec((G, LANES), lambda b, g, n: (g, rmap(b, g, n))),
            pl.BlockSpec((None, 1, G, LANES, LANES), lambda b, g, n: (layer, b, g, 0, 0)),
            pl.BlockSpec((1, LANES), lambda b, g, n: (0, 0)),
        ],
        out_specs=[
            pl.BlockSpec((CHUNK, v_w), lambda b, g, n: (rmap(b, g, n), g)),
            pl.BlockSpec((1, G, LANES, LANES), lambda b, g, n: (b, g, 0, 0)),
        ],
        out_shape=[jax.ShapeDtypeStruct((rows, HV * LANES), BF16),
                   jax.ShapeDtypeStruct((B, HV, LANES, LANES), F32)],
        scratch_shapes=[pltpu.VMEM((G, LANES, LANES), F32)],
        compiler_params=_cparams(("parallel", "parallel", "arbitrary"), 32),
        name="gdn_recurrence",
    )(qkv, qkv, qkv, zsrc, beta, gc, gct, s0, norm_g.reshape(1, LANES))


def _rope_kernel(x_ref, cos_ref, sin_ref, qb_ref, kf_ref, kb_ref, vb_ref, *, n_q, n_k, q_scale):
    cos = cos_ref[...]
    sin = sin_ref[...]
    for g in range(n_q + n_k):
        x = x_ref[:, g * LANES:(g + 1) * LANES]
        y = x * cos + pltpu.roll(x, LANES // 2, axis=1) * sin
        if g < n_q:
            if q_scale != 1.0:
                y = y * q_scale
            qb_ref[:, g * LANES:(g + 1) * LANES] = y.astype(BF16)
        else:
            kf_ref[:, (g - n_q) * LANES:(g - n_q + 1) * LANES] = y
            kb_ref[:, (g - n_q) * LANES:(g - n_q + 1) * LANES] = y.astype(BF16)
    vb_ref[...] = x_ref[:, (n_q + n_k) * LANES:].astype(BF16)


def rope_split(qkv, cos2, sin2, n_qk_ch, n_v_ch, q_scale):
    M = qkv.shape[0]
    Lc = cos2.shape[0]
    tr = min(M, 256)
    nb = Lc // tr
    n_q = n_qk_ch // LANES
    return pl.pallas_call(
        functools.partial(_rope_kernel, n_q=n_q, n_k=n_q, q_scale=q_scale),
        grid=(M // tr,),
        in_specs=[pl.BlockSpec((tr, qkv.shape[1]), lambda i: (i, 0)),
                  pl.BlockSpec((tr, LANES), lambda i: (i % nb, 0)),
                  pl.BlockSpec((tr, LANES), lambda i: (i % nb, 0))],
        out_specs=[pl.BlockSpec((tr, n_qk_ch), lambda i: (i, 0)),
                   pl.BlockSpec((tr, n_qk_ch), lambda i: (i, 0)),
                   pl.BlockSpec((tr, n_qk_ch), lambda i: (i, 0)),
                   pl.BlockSpec((tr, n_v_ch), lambda i: (i, 0))],
        out_shape=[jax.ShapeDtypeStruct((M, n_qk_ch), BF16),
                   jax.ShapeDtypeStruct((M, n_qk_ch), F32),
                   jax.ShapeDtypeStruct((M, n_qk_ch), BF16),
                   jax.ShapeDtypeStruct((M, n_v_ch), BF16)],
        compiler_params=_cparams(("parallel",), 40),
        name="rope_split",
    )(qkv, cos2, sin2)


def _rope_tables(pos):
    half = LANES // 2
    inv = 1.0 / (ROPE_THETA ** (jnp.arange(half, dtype=F32) * (2.0 / LANES)))
    ang = pos.astype(F32)[:, None] * inv[None, :]
    cos, sin = jnp.cos(ang), jnp.sin(ang)
    return jnp.concatenate([cos, cos], axis=-1), jnp.concatenate([-sin, sin], axis=-1)


def _diff_lambda_in_kernel(lam_ref, lam_init):
    lf = lam_ref[...]
    s1 = jnp.sum(lf[0:1, :] * lf[1:2, :], axis=-1, keepdims=True)
    s2 = jnp.sum(lf[2:3, :] * lf[3:4, :], axis=-1, keepdims=True)
    return jnp.exp(s1) - jnp.exp(s2) + lam_init


def _subln(o, g_ref, lam_init, eps=1e-5):
    y = o * lax.rsqrt(jnp.mean(o * o, axis=-1, keepdims=True) + eps)
    return (y * g_ref[...]) * (1.0 - lam_init)


def _flash_kernel(q_ref, k_ref, v_ref, lam_ref, g_ref, o_ref, m_scr, l_scr, acc_scr, *, T, lam_init):
    qi = pl.program_id(2)
    DH = LANES
    m_scr[...] = jnp.full(m_scr.shape, -jnp.inf, F32)
    l_scr[...] = jnp.zeros(l_scr.shape, F32)
    acc_scr[...] = jnp.zeros(acc_scr.shape, F32)
    q = q_ref[...]
    n_lt = T // LANES

    def tile(kt, masked):
        r0 = pl.multiple_of(kt * T, T)
        k = k_ref[pl.ds(r0, T), :]
        v = v_ref[pl.ds(r0, T), :]
        for c in range(2):
            s = lax.dot_general(q[:, c * DH:(c + 1) * DH], k[:, c * DH:(c + 1) * DH],
                                (((1,), (1,)), ((), ())), preferred_element_type=F32)
            if masked:
                rr = lax.broadcasted_iota(jnp.int32, (T, T), 0)
                cc = lax.broadcasted_iota(jnp.int32, (T, T), 1)
                s = jnp.where(cc <= rr, s, -jnp.inf)
            m_old = m_scr[c]
            m_new = jnp.maximum(m_old, jnp.max(s, axis=-1, keepdims=True))
            corr = jnp.exp2(m_old - m_new)
            e = [jnp.exp2(s[:, j * LANES:(j + 1) * LANES] - m_new) for j in range(n_lt)]
            rs = e[0]
            for j in range(1, n_lt):
                rs = rs + e[j]
            l_scr[c] = l_scr[c] * corr + jnp.sum(rs, axis=-1, keepdims=True)
            eb = jnp.concatenate([x.astype(BF16) for x in e], axis=1)
            pv = jnp.dot(eb, v, preferred_element_type=F32)
            acc_scr[c] = acc_scr[c] * jnp.concatenate([corr, corr], axis=1) + pv
            m_scr[c] = m_new

    def body(kt, carry):
        tile(kt, False)
        return carry

    lax.fori_loop(0, qi, body, 0)
    tile(qi, True)
    lam = _diff_lambda_in_kernel(lam_ref, lam_init)
    l0 = jnp.concatenate([l_scr[0], l_scr[0]], axis=1)
    l1 = jnp.concatenate([l_scr[1], l_scr[1]], axis=1)
    o = acc_scr[0] / l0 - lam * (acc_scr[1] / l1)
    o_ref[...] = _subln(o, g_ref, lam_init).astype(o_ref.dtype)


def flash_diff_attention(qb, kb, vb, lam_rows, subln, B, L, H, lam_init):
    T = 512
    W = 2 * LANES
    nq = L // T
    return pl.pallas_call(
        functools.partial(_flash_kernel, T=T, lam_init=lam_init),
        grid=(B, H, nq),
        in_specs=[pl.BlockSpec((T, W), lambda b, h, i: (b * nq + i, h)),
                  pl.BlockSpec((L, W), lambda b, h, i: (b, h)),
                  pl.BlockSpec((L, W), lambda b, h, i: (b, h)),
                  pl.BlockSpec((4, LANES), lambda b, h, i: (0, 0)),
                  pl.BlockSpec((1, W), lambda b, h, i: (0, 0))],
        out_specs=pl.BlockSpec((T, W), lambda b, h, i: (b * nq + i, h)),
        out_shape=jax.ShapeDtypeStruct((B * L, H * W), BF16),
        scratch_shapes=[pltpu.VMEM((2, T, LANES), F32), pltpu.VMEM((2, T, LANES), F32), pltpu.VMEM((2, T, W), F32)],
        compiler_params=_cparams(("parallel", "parallel", "arbitrary"), 40),
        name="flash_diff_attention",
    )(qb, kb, vb, lam_rows, subln.reshape(1, W))


def _paged_kernel(pt_ref, q_ref, kn_ref, vn_ref, lam_ref, g_ref, *rest, P, H, T, TP, n_steps, lam_init):
    k_refs = rest[:P]
    v_refs = rest[P:2 * P]
    o_ref = rest[2 * P]
    m_scr, l_scr, acc_scr = rest[2 * P + 1:]
    p = pl.program_id(1)
    NR = H * T
    q = q_ref[0]
    nt = functools.partial(lax.dot_general, dimension_numbers=(((1,), (1,)), ((), ())),
                           preferred_element_type=F32)

    def masked(s, causal):
        row = lax.broadcasted_iota(jnp.int32, s.shape, 0) % NR
        col = lax.broadcasted_iota(jnp.int32, s.shape, 1)
        ok = col % H == row // T
        if causal:
            ok = jnp.logical_and(ok, col // H <= row % T)
        return jnp.where(ok, s, -jnp.inf)

    def update(s, v):
        m_old = m_scr[...]
        m_new = jnp.maximum(m_old, jnp.max(s, axis=-1, keepdims=True))
        corr = jnp.exp(m_old - m_new)
        e = jnp.exp(s - m_new)
        l_scr[...] = l_scr[...] * corr + jnp.sum(e, axis=-1, keepdims=True)
        m_scr[...] = m_new
        acc_scr[...] = acc_scr[...] * corr + jnp.dot(e.astype(BF16), v, preferred_element_type=F32)

    @pl.when(p == 0)
    def _():
        m_scr[...] = jnp.full(m_scr.shape, -jnp.inf, F32)
        l_scr[...] = jnp.zeros(l_scr.shape, F32)
        acc_scr[...] = jnp.zeros(acc_scr.shape, F32)
        kn = kn_ref[0]
        s = jnp.concatenate([nt(q[c], kn[c]) for c in range(2)], axis=0)
        update(masked(s, True), vn_ref[0])

    ss = []
    for c in range(2):
        kc = jnp.concatenate([k_refs[r][pl.ds(c, TP * H, stride=2), :] for r in range(P)], axis=0)
        ss.append(nt(q[c], kc.astype(BF16)))
    v = jnp.concatenate([v_refs[r][...] for r in range(P)], axis=0).astype(BF16)
    update(masked(jnp.concatenate(ss, axis=0), False), v)

    @pl.when(p == n_steps - 1)
    def _():
        lam = _diff_lambda_in_kernel(lam_ref, lam_init)
        po = acc_scr[...] / l_scr[...]
        o = po[:NR, :] - lam * po[NR:, :]
        o_ref[0] = _subln(o, g_ref, lam_init).astype(o_ref.dtype)


def paged_diff_attention(qs, kn, vn, k_pool, v_pool, layer, page_table, lam_rows, subln, H, T, lam_init):
    B = qs.shape[0]
    n_pages = page_table.shape[1]
    TP = k_pool.shape[2] // (2 * H)
    P = 4
    n_steps = n_pages // P
    W = 2 * LANES
    NR = H * T

    def kmap(r):
        return lambda b, p, pt: (layer, pt[b, p * P + r], 0, 0)

    in_specs = [pl.BlockSpec((1, 2, NR, LANES), lambda b, p, pt: (b, 0, 0, 0)),
                pl.BlockSpec((1, 2, NR, LANES), lambda b, p, pt: (b, 0, 0, 0)),
                pl.BlockSpec((1, NR, W), lambda b, p, pt: (b, 0, 0)),
                pl.BlockSpec((4, LANES), lambda b, p, pt: (0, 0)),
                pl.BlockSpec((1, W), lambda b, p, pt: (0, 0))]
    in_specs += [pl.BlockSpec((None, None, TP * 2 * H, LANES), kmap(r)) for r in range(P)]
    in_specs += [pl.BlockSpec((None, None, TP * H, W), kmap(r)) for r in range(P)]
    return pl.pallas_call(
        functools.partial(_paged_kernel, P=P, H=H, T=T, TP=TP, n_steps=n_steps, lam_init=lam_init),
        grid_spec=pltpu.PrefetchScalarGridSpec(
            num_scalar_prefetch=1,
            grid=(B, n_steps),
            in_specs=in_specs,
            out_specs=pl.BlockSpec((1, NR, W), lambda b, p, pt: (b, 0, 0)),
            scratch_shapes=[pltpu.VMEM((2 * NR, 1), F32), pltpu.VMEM((2 * NR, 1), F32),
                            pltpu.VMEM((2 * NR, W), F32)],
        ),
        out_shape=jax.ShapeDtypeStruct((B, NR, W), BF16),
        compiler_params=_cparams(("parallel", "arbitrary"), 48),
        name="paged_diff_attention",
    )(page_table, qs, kn, vn, lam_rows, subln.reshape(1, W), *([k_pool] * P), *([v_pool] * P))


def _mem_attn_kernel(q_ref, k_ref, v_ref, o_ref, *, H, scale):
    q = q_ref[0]
    k = k_ref[0].astype(BF16)
    v = v_ref[0].astype(BF16)
    for h in range(H):
        sl = slice(h * LANES, (h + 1) * LANES)
        s = lax.dot_general(q[:, sl], k[:, sl], (((1,), (1,)), ((), ())), preferred_element_type=F32) * scale
        m = jnp.max(s, axis=-1, keepdims=True)
        e = jnp.exp(s - m)
        p = e / jnp.sum(e, axis=-1, keepdims=True)
        o_ref[0, :, sl] = jnp.dot(p.astype(BF16), v[:, sl], preferred_element_type=F32).astype(o_ref.dtype)


def mem_attention(q, mk, mv, H):
    B, L, D = q.shape
    Mm = mk.shape[1]
    tq = min(L, 512)
    return pl.pallas_call(
        functools.partial(_mem_attn_kernel, H=H, scale=LANES ** -0.5),
        grid=(B, L // tq),
        in_specs=[pl.BlockSpec((1, tq, D), lambda b, i: (b, i, 0)),
                  pl.BlockSpec((1, Mm, D), lambda b, i: (b, 0, 0)),
                  pl.BlockSpec((1, Mm, D), lambda b, i: (b, 0, 0))],
        out_specs=pl.BlockSpec((1, tq, D), lambda b, i: (b, i, 0)),
        out_shape=jax.ShapeDtypeStruct((B, L, D), BF16),
        compiler_params=_cparams(("parallel", "parallel"), 24),
        name="mem_attention",
    )(q, mk, mv)


def _gdn_mixer(hn, B, L, conv_state, s0, j, w_in, w_ba, conv_w, a_log, dt_bias, norm_g, w_out, x_res, dims):
    HK, HV, n_qk, n_conv = dims
    h = matmul(hn, w_in, j, n_out=n_conv + HV * LANES)
    ba = matmul(hn, w_ba, j)
    q_scale = LANES ** -0.5
    if conv_state is None:
        qkv = conv_prompt(h, conv_w, B, L, n_qk, n_conv, q_scale)
        new_conv = h.reshape(B, L, -1)[:, L - (CONV_W - 1):, :n_conv]
        NC = L // CHUNK
        zsrc, z_col0 = h, n_conv
        beta, gc, gct = gdn_gates(ba, a_log, dt_bias, CHUNK, HV)
    else:
        u = h[:, :n_conv].reshape(B, L, n_conv)
        ut = jnp.swapaxes(u, 0, 1)
        st = jnp.swapaxes(conv_state, 0, 1)
        y = conv_sample(ut, st, conv_w, n_qk, q_scale)
        new_conv = jnp.concatenate([conv_state, u], axis=1)[:, L:, :]
        padrows = lambda t: jnp.pad(t.reshape(B, L, -1), ((0, 0), (0, CHUNK - L), (0, 0))).reshape(B * CHUNK, -1)
        qkv = padrows(jnp.swapaxes(y, 0, 1))
        zsrc, z_col0 = padrows(h[:, n_conv:]), 0
        NC = 1
        beta, gc, gct = gdn_gates(padrows(ba), a_log, dt_bias, L, HV)
    og, s_new = gdn_recurrence(qkv, zsrc, z_col0, beta, gc, gct, s0, j if s0.shape[0] > 1 else 0, norm_g,
                               B, NC, HK, HV)
    if conv_state is not None:
        og = og.reshape(B, CHUNK, -1)[:, :L].reshape(B * L, -1)
    return matmul(og, w_out, j, res=x_res), new_conv, s_new


def _forward(x, mem_kv_fn, B, L, pos, gdn_state, attn_fn, attn_q_scale, params):
    (norm_mix, gdn_w_in, gdn_w_ba, gdn_conv_w, gdn_a_log, gdn_dt_bias, gdn_norm, gdn_w_out, attn_w_qkv,
     attn_lambda, attn_subln, attn_w_o, norm_xattn, w_xq, w_xo, norm_ffn, w_up, w_down, norm_final) = params
    depth = norm_mix.shape[0]
    D = x.shape[1]
    HV = gdn_a_log.shape[1]
    HK = HV // 2
    n_qk = HK * LANES
    n_conv = gdn_conv_w.shape[2]
    XH = w_xq.shape[2] // LANES
    cos2, sin2 = _rope_tables(pos)
    if L < 256:
        cos2, sin2 = jnp.tile(cos2, (B, 1)), jnp.tile(sin2, (B, 1))
    Lq = max(L, SLOT_ROWS)
    convs, deltas, ks, vs = [], [], [], []
    for i in range(depth):
        j = i // 2
        hn = rmsnorm(x, norm_mix[i], BF16)
        if i % 2 == 0:
            conv_state, s0 = gdn_state(j)
            x, new_conv, s_new = _gdn_mixer(hn, B, L, conv_state, s0, j, gdn_w_in, gdn_w_ba, gdn_conv_w[j],
                                            gdn_a_log[j], gdn_dt_bias[j], gdn_norm[j], gdn_w_out, x,
                                            (HK, HV, n_qk, n_conv))
            convs.append(new_conv)
            deltas.append(s_new)
        else:
            lam_init = 0.8 - 0.6 * math.exp(-0.3 * i)
            n_qk_a = attn_w_qkv.shape[2] // 3
            qkv = matmul(hn, attn_w_qkv, j)
            qb, kf, kb, vb = rope_split(qkv, cos2, sin2, n_qk_a, n_qk_a, attn_q_scale)
            oa = attn_fn(j, qb, kb, vb, attn_lambda[j], attn_subln[j], lam_init)
            x = matmul(oa, attn_w_o, j, res=x)
            ks.append(kf)
            vs.append(qkv[:, 2 * n_qk_a:])
        hx = rmsnorm(x, norm_xattn[i], BF16)
        q = matmul(hx, w_xq, i, out_dtype=BF16).reshape(B, L, -1)
        mk, mv = mem_kv_fn(i)
        om = mem_attention(jnp.pad(q, ((0, 0), (0, Lq - L), (0, 0))), mk, mv, XH)[:, :L]
        x = matmul(om.reshape(B * L, -1), w_xo, i, res=x)
        hf = rmsnorm(x, norm_ffn[i], BF16)
        up = matmul(hf, w_up, i, out_dtype=BF16, act="relu2")
        x = matmul(up, w_down, i, res=x)
    y = rmsnorm(x, norm_final, F32)
    return y, convs, deltas, ks, vs


def kernel(x_prompt, x_sample, mem_prompt, state_conv, state_delta, cache_k, cache_v, cache_mem_k, cache_mem_v, page_table, norm_mix, gdn_w_in, gdn_conv_w, gdn_a_log, gdn_dt_bias, gdn_norm, gdn_w_out, attn_w_qkv, attn_lambda, attn_subln, attn_w_o, norm_xattn, norm_mem, w_xq, w_xkv, w_xo, norm_ffn, w_up, w_down, norm_final):
    Bp, Lp, D = x_prompt.shape
    Bs, Ls, _ = x_sample.shape
    depth = norm_mix.shape[0]
    HV = gdn_a_log.shape[1]
    n_conv = gdn_conv_w.shape[2]
    n_attn, n_pool, TP, H = cache_k.shape[:4]
    W = 2 * LANES
    XH = w_xq.shape[2] // LANES
    Mm = mem_prompt.shape[1]
    past_len = page_table.shape[1] * TP

    n_main = n_conv + HV * LANES
    padc = lambda w: jnp.pad(w, ((0, 0), (0, 0), (0, LANES - HV)))
    gdn_w_ba = jnp.concatenate([padc(gdn_w_in[:, :, n_main:n_main + HV]), padc(gdn_w_in[:, :, n_main + HV:])], axis=-1)

    params = (norm_mix, gdn_w_in, gdn_w_ba, gdn_conv_w, gdn_a_log, gdn_dt_bias, gdn_norm, gdn_w_out, attn_w_qkv,
              attn_lambda, attn_subln, attn_w_o, norm_xattn, w_xq, w_xo, norm_ffn, w_up, w_down, norm_final)

    p_mk, p_mv = [], []

    def prompt_mem(i):
        mn = rmsnorm(mem_prompt.reshape(Bp * Mm, D), norm_mem[i], BF16)
        kv = matmul(mn, w_xkv, i).reshape(Bp, Mm, 2 * XH * LANES)
        mk, mv = kv[:, :, :XH * LANES], kv[:, :, XH * LANES:]
        p_mk.append(mk.reshape(Bp, Mm, XH, LANES))
        p_mv.append(mv.reshape(Bp, Mm, XH, LANES))
        return mk, mv

    def prompt_attn(j, qb, kb, vb, lam_rows, subln, lam_init):
        return flash_diff_attention(qb, kb, vb, lam_rows, subln, Bp, Lp, H, lam_init)

    zero_state = jnp.zeros((1, Bp, HV, LANES, LANES), F32)
    yp, p_conv, p_delta, p_k, p_v = _forward(
        x_prompt.reshape(Bp * Lp, D), prompt_mem, Bp, Lp, jnp.arange(Lp), lambda j: (None, zero_state),
        prompt_attn, LANES ** -0.5 * math.log2(math.e), params)

    k_pool = cache_k.reshape(n_attn, n_pool, TP * H * 2, LANES)
    v_pool = cache_v.reshape(n_attn, n_pool, TP * H, W)

    def sample_attn(j, qb, kb, vb, lam_rows, subln, lam_init):
        qs = jnp.transpose(qb.reshape(Bs, Ls, H, 2, LANES), (0, 3, 2, 1, 4)).reshape(Bs, 2, H * Ls, LANES)
        kn = jnp.transpose(kb.reshape(Bs, Ls, H, 2, LANES), (0, 3, 1, 2, 4)).reshape(Bs, 2, Ls * H, LANES)
        oa = paged_diff_attention(qs, kn, vb.reshape(Bs, Ls * H, W), k_pool, v_pool, j, page_table,
                                  lam_rows, subln, H, Ls, lam_init)
        return jnp.swapaxes(oa.reshape(Bs, H, Ls, W), 1, 2).reshape(Bs * Ls, H * W)

    def sample_gdn_state(j):
        return state_conv[j], state_delta

    ys, s_conv, s_delta, s_k, s_v = _forward(
        x_sample.reshape(Bs * Ls, D), lambda i: (cache_mem_k[i].reshape(Bs, Mm, -1), cache_mem_v[i].reshape(Bs, Mm, -1)),
        Bs, Ls, past_len + jnp.arange(Ls), sample_gdn_state, sample_attn, LANES ** -0.5, params)

    return (yp.reshape(Bp, Lp, D), ys.reshape(Bs, Ls, D),
            jnp.stack(p_conv), jnp.stack(p_delta), jnp.stack(s_conv), jnp.stack(s_delta),
            jnp.stack(p_k).reshape(n_attn, Bp, Lp, H, 2, LANES), jnp.stack(p_v).reshape(n_attn, Bp, Lp, H, W),
            jnp.stack(s_k).reshape(n_attn, Bs, Ls, H, 2, LANES), jnp.stack(s_v).reshape(n_attn, Bs, Ls, H, W),
            jnp.stack(p_mk), jnp.stack(p_mv))
```

```python
import functools
import math

import jax
import jax.numpy as jnp
from jax import lax
from jax.experimental import pallas as pl
from jax.experimental.pallas import tpu as pltpu

F32 = jnp.float32
BF16 = jnp.bfloat16

LANES = 128
CHUNK = 64
CONV_W = 4
EPS = 1e-6
ROPE_THETA = 10000.0
VMEM_CAP = 60 * 1024 * 1024
SLOT_ROWS = 16


def _cparams(sem, vmem_mb):
    return pltpu.CompilerParams(dimension_semantics=sem, vmem_limit_bytes=min(VMEM_CAP, vmem_mb * 1024 * 1024))


def _rmsnorm_kernel(x_ref, g_ref, o_ref, *, eps):
    x = x_ref[...]
    y = x * lax.rsqrt(jnp.mean(x * x, axis=-1, keepdims=True) + eps)
    o_ref[...] = (y * g_ref[...]).astype(o_ref.dtype)


def rmsnorm(x, gain, out_dtype, eps=EPS):
    M, D = x.shape
    tm = min(M, 512)
    return pl.pallas_call(
        functools.partial(_rmsnorm_kernel, eps=eps),
        grid=(M // tm,),
        in_specs=[pl.BlockSpec((tm, D), lambda i: (i, 0)), pl.BlockSpec((1, D), lambda i: (0, 0))],
        out_specs=pl.BlockSpec((tm, D), lambda i: (i, 0)),
        out_shape=jax.ShapeDtypeStruct((M, D), out_dtype),
        compiler_params=_cparams(("parallel",), 32),
        name="rmsnorm",
    )(x, gain.reshape(1, D))


def _mm_kernel(*refs, nk, act, has_res, w_is_t):
    if has_res:
        a_ref, w_ref, r_ref, o_ref = refs[:4]
        scratch = refs[4:]
    else:
        a_ref, w_ref, o_ref = refs[:3]
        r_ref = None
        scratch = refs[3:]
    contract_w = 1 if w_is_t else 0
    p = lax.dot_general(a_ref[...].astype(BF16), w_ref[...].astype(BF16), (((1,), (contract_w,)), ((), ())),
                        preferred_element_type=F32)

    def finish(acc):
        if act == "relu2":
            r = jnp.maximum(acc, 0.0)
            acc = r * r
        if has_res:
            acc = acc + r_ref[...]
        o_ref[...] = acc.astype(o_ref.dtype)

    if nk == 1:
        finish(p)
    else:
        acc_ref = scratch[0]
        k = pl.program_id(2)

        @pl.when(k == 0)
        def _():
            acc_ref[...] = p

        @pl.when(k > 0)
        def _():
            acc_ref[...] += p

        @pl.when(k == nk - 1)
        def _():
            finish(acc_ref[...])


def matmul(a, w, layer, *, n_out=None, out_dtype=F32, act=None, res=None, w_is_t=False):
    M, K = a.shape
    N = n_out if n_out is not None else w.shape[1 if w_is_t else 2]
    if M >= 2048 and K <= 2048:
        tm, tn, tk = 2048, min(N, 512), K
    elif M >= 2048 and K <= 4096:
        tm, tn, tk = 1024, min(N, 512), K
    elif M >= 2048:
        tm, tn, tk = 2048, min(N, 512), 2048
    else:
        tm, tn, tk = M, min(N, 1024), min(K, 2048)
    nk = K // tk
    if w_is_t:
        w_spec = pl.BlockSpec((None, tn, tk), lambda i, j, k: (layer, j, k))
    else:
        w_spec = pl.BlockSpec((None, tk, tn), lambda i, j, k: (layer, k, j))
    in_specs = [pl.BlockSpec((tm, tk), lambda i, j, k: (i, k)), w_spec]
    args = [a, w]
    if res is not None:
        in_specs.append(pl.BlockSpec((tm, tn), lambda i, j, k: (i, j)))
        args.append(res)
    out_b = jnp.dtype(out_dtype).itemsize
    est = 2 * (tm * tk * a.dtype.itemsize + tk * tn * 4 + tm * tn * out_b + (tm * tn * 4 if res is not None else 0))
    est += tm * tn * 4 * (2 if nk > 1 else 1) + tk * tn * 2
    return pl.pallas_call(
        functools.partial(_mm_kernel, nk=nk, act=act, has_res=res is not None, w_is_t=w_is_t),
        grid=(M // tm, N // tn, nk),
        in_specs=in_specs,
        out_specs=pl.BlockSpec((tm, tn), lambda i, j, k: (i, j)),
        out_shape=jax.ShapeDtypeStruct((M, N), out_dtype),
        scratch_shapes=[pltpu.VMEM((tm, tn), F32)] if nk > 1 else [],
        compiler_params=_cparams(("parallel", "parallel", "arbitrary"), est // (1024 * 1024) + 8),
        name="matmul",
    )(*args)


def _mm_tail_kernel(a_ref, w_ref, o_ref):
    p = lax.dot_general(a_ref[...], w_ref[...].astype(BF16), (((1,), (1,)), ((), ())), preferred_element_type=F32)
    o_ref[...] = jnp.concatenate([p, jnp.zeros((p.shape[0], LANES - p.shape[1]), F32)], axis=1)


def matmul_tail(a, wt, layer, row0):
    M, K = a.shape
    n_tail = wt.shape[1] - row0
    tm = min(M, 2048)
    return pl.pallas_call(
        _mm_tail_kernel,
        grid=(M // tm,),
        in_specs=[pl.BlockSpec((tm, K), lambda i: (i, 0)),
                  pl.BlockSpec((None, n_tail, K), lambda i: (layer, row0 // n_tail, 0))],
        out_specs=pl.BlockSpec((tm, LANES), lambda i: (i, 0)),
        out_shape=jax.ShapeDtypeStruct((M, LANES), F32),
        compiler_params=_cparams(("parallel",), 32),
        name="matmul_tail",
    )(a, wt)


def _l2_groups(y, scale):
    outs = []
    for g in range(y.shape[-1] // LANES):
        yg = y[:, g * LANES:(g + 1) * LANES]
        ss = jnp.sum(yg * yg, axis=-1, keepdims=True)
        outs.append(yg * (lax.rsqrt(ss + 1e-6) * scale))
    return outs


def _conv_post(y, o_ref, rows, j, n_q, n_k, q_scale):
    y = y * jax.nn.sigmoid(y)

    @pl.when(j < n_q)
    def _():
        for g, yg in enumerate(_l2_groups(y, q_scale)):
            o_ref[rows, g * LANES:(g + 1) * LANES] = yg

    @pl.when(jnp.logical_and(j >= n_q, j < n_q + n_k))
    def _():
        for g, yg in enumerate(_l2_groups(y, 1.0)):
            o_ref[rows, g * LANES:(g + 1) * LANES] = yg

    @pl.when(j >= n_q + n_k)
    def _():
        o_ref[rows, :] = y


def _conv_prompt_kernel(x_ref, w_ref, o_ref, *, L, TR, n_q, n_k, q_scale):
    j = pl.program_id(1)
    w = w_ref[...]

    def run(l2_scale):
        def body(i, carry):
            r0 = pl.multiple_of(i * TR, TR)
            xa = x_ref[pl.ds(r0, TR), :]
            pstart = pl.multiple_of(jnp.maximum(r0 - 8, 0), 8)
            xp = jnp.where(i == 0, 0.0, x_ref[pl.ds(pstart, 8), :])
            xx = jnp.concatenate([xp, xa], axis=0)
            y = xa * w[3:4, :]
            for s in range(1, CONV_W):
                y = y + pltpu.roll(xx, s, axis=0)[8:, :] * w[3 - s:4 - s, :]
            y = y * jax.nn.sigmoid(y)
            if l2_scale is None:
                o_ref[pl.ds(r0, TR), :] = y
            else:
                for g, yg in enumerate(_l2_groups(y, l2_scale)):
                    o_ref[pl.ds(r0, TR), g * LANES:(g + 1) * LANES] = yg
            return carry

        lax.fori_loop(0, L // TR, body, 0)

    pl.when(j < n_q)(lambda: run(q_scale))
    pl.when(jnp.logical_and(j >= n_q, j < n_q + n_k))(lambda: run(1.0))
    pl.when(j >= n_q + n_k)(lambda: run(None))


def conv_prompt(h, conv_w, B, L, n_qk_ch, n_ch, q_scale):
    tc = 256
    n_q = n_qk_ch // tc
    return pl.pallas_call(
        functools.partial(_conv_prompt_kernel, L=L, TR=256, n_q=n_q, n_k=n_q, q_scale=q_scale),
        grid=(B, n_ch // tc),
        in_specs=[pl.BlockSpec((L, tc), lambda b, j: (b, j)), pl.BlockSpec((CONV_W, tc), lambda b, j: (0, j))],
        out_specs=pl.BlockSpec((L, tc), lambda b, j: (b, j)),
        out_shape=jax.ShapeDtypeStruct((B * L, n_ch), F32),
        compiler_params=_cparams(("parallel", "parallel"), 40),
        name="conv_prompt",
    )(h, conv_w)


def _conv_sample_kernel(u_ref, st_ref, w_ref, o_ref, *, T, n_q, n_k, q_scale):
    j = pl.program_id(0)
    w = w_ref[...]
    xc = [st_ref[i] for i in range(CONV_W - 1)] + [u_ref[t] for t in range(T)]
    for t in range(T):
        y = xc[t] * w[0:1, :]
        for i in range(1, CONV_W):
            y = y + xc[t + i] * w[i:i + 1, :]
        _conv_post(y, o_ref.at[t], slice(None), j, n_q, n_k, q_scale)


def conv_sample(u, state, conv_w, n_qk_ch, q_scale):
    T, B, C = u.shape
    tc = 1024
    n_q = n_qk_ch // tc
    return pl.pallas_call(
        functools.partial(_conv_sample_kernel, T=T, n_q=n_q, n_k=n_q, q_scale=q_scale),
        grid=(C // tc,),
        in_specs=[pl.BlockSpec((T, B, tc), lambda j: (0, 0, j)),
                  pl.BlockSpec((CONV_W - 1, B, tc), lambda j: (0, 0, j)),
                  pl.BlockSpec((CONV_W, tc), lambda j: (0, j))],
        out_specs=pl.BlockSpec((T, B, tc), lambda j: (0, 0, j)),
        out_shape=jax.ShapeDtypeStruct((T, B, C), F32),
        compiler_params=_cparams(("parallel",), 16),
        name="conv_sample",
    )(u, state, conv_w)


def _gates_kernel(ba_ref, alog_ref, dtb_ref, beta_ref, gc_ref, gct_ref, *, valid, n_heads):
    b = ba_ref[...]
    a = pltpu.roll(b, LANES - n_heads, axis=1)
    tb = b.shape[0]
    rc = lax.broadcasted_iota(jnp.int32, (tb, LANES), 0) % CHUNK
    ok = rc < valid
    beta = jnp.where(ok, jax.nn.sigmoid(b), 0.0)
    z = a + dtb_ref[...]
    softplus = jnp.maximum(z, 0.0) + jnp.log1p(jnp.exp(-jnp.abs(z)))
    g = jnp.where(ok, -jnp.exp(alog_ref[...]) * softplus, 0.0)
    s = 1
    while s < CHUNK:
        g = g + jnp.where(rc >= s, pltpu.roll(g, s, axis=0), 0.0)
        s *= 2
    beta_ref[...] = beta
    gc_ref[...] = g
    g3 = g.reshape(tb // CHUNK, CHUNK, LANES)
    gz = jnp.concatenate([g3, jnp.zeros_like(g3)], axis=1).reshape(2 * tb, LANES)
    gct_ref[...] = gz.T[:n_heads, :]


def gdn_gates(ba, a_log, dt_bias, valid, n_heads):
    M = ba.shape[0]
    tb = min(M, 512)
    pad = lambda v: jnp.pad(v.astype(F32), (0, LANES - n_heads)).reshape(1, LANES)
    return pl.pallas_call(
        functools.partial(_gates_kernel, valid=valid, n_heads=n_heads),
        grid=(M // tb,),
        in_specs=[pl.BlockSpec((tb, LANES), lambda i: (i, 0)),
                  pl.BlockSpec((1, LANES), lambda i: (0, 0)),
                  pl.BlockSpec((1, LANES), lambda i: (0, 0))],
        out_specs=[pl.BlockSpec((tb, LANES), lambda i: (i, 0)),
                   pl.BlockSpec((tb, LANES), lambda i: (i, 0)),
                   pl.BlockSpec((n_heads, 2 * tb), lambda i: (0, i))],
        out_shape=[jax.ShapeDtypeStruct((M, LANES), F32),
                   jax.ShapeDtypeStruct((M, LANES), F32),
                   jax.ShapeDtypeStruct((n_heads, 2 * M), F32)],
        compiler_params=_cparams(("parallel",), 16),
        name="gdn_gates",
    )(ba, pad(a_log), pad(dt_bias))


def _bdot(a, b):
    return jnp.dot(a.astype(BF16), b.astype(BF16), preferred_element_type=F32)


def _bdot_nt(a, b):
    return lax.dot_general(a.astype(BF16), b.astype(BF16), (((1,), (1,)), ((), ())), preferred_element_type=F32)


def _bdot_tn(a, b):
    return lax.dot_general(a.astype(BF16), b.astype(BF16), (((0,), (0,)), ((), ())), preferred_element_type=F32)


def _gdn_kernel(q_ref, k_ref, v_ref, z_ref, beta_ref, gc_ref, gr_ref, s0_ref, ng_ref, o_ref, so_ref, s_scr,
                *, G, NC, eps):
    hg = pl.program_id(1)
    n = pl.program_id(2)
    C = CHUNK

    @pl.when(n == 0)
    def _():
        s_scr[...] = s0_ref[0]

    row = lax.broadcasted_iota(jnp.int32, (C, C), 0)
    col = lax.broadcasted_iota(jnp.int32, (C, C), 1)
    incl = row >= col
    strict = row > col
    eye = (row == col).astype(F32)
    lane = lax.broadcasted_iota(jnp.int32, (C, LANES), 1)
    beta_all = beta_ref[...]
    gc_all = gc_ref[...]
    ng = ng_ref[...]
    heads = range(G)
    dot = functools.partial(jnp.dot, preferred_element_type=F32)

    qs = [q_ref[:, kh * LANES:(kh + 1) * LANES] for kh in range(G // 2)]
    ks = [k_ref[:, kh * LANES:(kh + 1) * LANES] for kh in range(G // 2)]
    qkk = [_bdot_nt(jnp.concatenate([qs[kh], ks[kh]], axis=0), ks[kh]) for kh in range(G // 2)]

    beta, eg, et, egl, dec, a, t = [], [], [], [], [], [], []
    for g in heads:
        sel = lane == hg * G + g
        b_g = jnp.sum(jnp.where(sel, beta_all, 0.0), axis=-1, keepdims=True)
        gc = jnp.sum(jnp.where(sel, gc_all, 0.0), axis=-1, keepdims=True)
        gl = gc[C - 1:C, :]
        gr = gr_ref[g:g + 1, :][:, :C]
        d_g = jnp.exp(jnp.where(incl, gc - gr, -jnp.inf))
        a_g = b_g * qkk[g // 2][C:] * jnp.where(strict, d_g, 0.0)
        beta.append(b_g)
        eg.append(jnp.exp(gc))
        et.append(jnp.exp(gl - gc))
        egl.append(jnp.exp(gl))
        dec.append(d_g)
        a.append(a_g)
        t.append(eye - jnp.where(row // 2 == col // 2, a_g, 0.0))

    s = 2
    while s < C:
        fmask = jnp.logical_and(row // (2 * s) == col // (2 * s), row // s != col // s)
        tb = [t[g].astype(BF16) for g in heads]
        x = [dot(tb[g], jnp.where(fmask, a[g], 0.0).astype(BF16)) for g in heads]
        t = [t[g] - dot(x[g].astype(BF16), tb[g]) for g in heads]
        s *= 2

    sol = []
    for g in heads:
        v = v_ref[:, g * LANES:(g + 1) * LANES]
        rhs = jnp.concatenate([beta[g] * v, (beta[g] * eg[g]) * ks[g // 2]], axis=1)
        sol.append(_bdot(t[g], rhs))
    S = [s_scr[g] for g in heads]
    wq = [_bdot(jnp.concatenate([sol[g][:, LANES:], qs[g // 2]], axis=0), S[g]) for g in heads]
    u = [sol[g][:, :LANES] - wq[g][:C] for g in heads]
    o1 = [_bdot(qkk[g // 2][:C] * dec[g], u[g]) for g in heads]
    for g in heads:
        s_scr[g] = egl[g] * S[g] + _bdot_tn(ks[g // 2] * et[g], u[g])
    for g in heads:
        o = eg[g] * wq[g][C:] + o1[g]
        on = o * lax.rsqrt(jnp.mean(o * o, axis=-1, keepdims=True) + eps) * ng
        zz = z_ref[:, g * LANES:(g + 1) * LANES]
        o_ref[:, g * LANES:(g + 1) * LANES] = (on * (zz * jax.nn.sigmoid(zz))).astype(o_ref.dtype)

    @pl.when(n == NC - 1)
    def _():
        so_ref[0] = s_scr[...]


def gdn_recurrence(qkv, zsrc, z_col0, beta, gc, gct, s0, layer, norm_g, B, NC, HK, HV, G=32):
    rows = B * NC * CHUNK
    GK = G // 2
    n_g = HV // G
    kq_w = GK * LANES
    v_w = G * LANES
    k_blk0 = (HK * LANES) // kq_w
    v_blk0 = (2 * HK * LANES) // v_w
    z_blk0 = z_col0 // v_w
    rmap = lambda b, g, n: b * NC + n
    return pl.pallas_call(
        functools.partial(_gdn_kernel, G=G, NC=NC, eps=EPS),
        grid=(B, n_g, NC),
        in_specs=[
            pl.BlockSpec((CHUNK, kq_w), lambda b, g, n: (rmap(b, g, n), g)),
            pl.BlockSpec((CHUNK, kq_w), lambda b, g, n: (rmap(b, g, n), k_blk0 + g)),
            pl.BlockSpec((CHUNK, v_w), lambda b, g, n: (rmap(b, g, n), v_blk0 + g)),
            pl.BlockSpec((CHUNK, v_w), lambda b, g, n: (rmap(b, g, n), z_blk0 + g)),
            pl.BlockSpec((CHUNK, LANES), lambda b, g, n: (rmap(b, g, n), 0)),
            pl.BlockSpec((CHUNK, LANES), lambda b, g, n: (rmap(b, g, n), 0)),
            pl.BlockSpec((G, LANES), lambda b, g, n: (g, rmap(b, g, n))),
            pl.BlockSpec((None, 1, G, LANES, LANES), lambda b, g, n: (layer, b, g, 0, 0)),
            pl.BlockSpec((1, LANES), lambda b, g, n: (0, 0)),
        ],
        out_specs=[
            pl.BlockSpec((CHUNK, v_w), lambda b, g, n: (rmap(b, g, n), g)),
            pl.BlockSpec((1, G, LANES, LANES), lambda b, g, n: (b, g, 0, 0)),
        ],
        out_shape=[jax.ShapeDtypeStruct((rows, HV * LANES), BF16),
                   jax.ShapeDtypeStruct((B, HV, LANES, LANES), F32)],
        scratch_shapes=[pltpu.VMEM((G, LANES, LANES), F32)],
        compiler_params=_cparams(("parallel", "parallel", "arbitrary"), 32),
        name="gdn_recurrence",
    )(qkv, qkv, qkv, zsrc, beta, gc, gct, s0, norm_g.reshape(1, LANES))


def _rope_kernel(x_ref, cos_ref, sin_ref, qb_ref, kf_ref, kb_ref, vb_ref, *, n_q, n_k, q_scale):
    cos = cos_ref[...]
    sin = sin_ref[...]
    for g in range(n_q + n_k):
        x = x_ref[:, g * LANES:(g + 1) * LANES]
        y = x * cos + pltpu.roll(x, LANES // 2, axis=1) * sin
        if g < n_q:
            if q_scale != 1.0:
                y = y * q_scale
            qb_ref[:, g * LANES:(g + 1) * LANES] = y.astype(BF16)
        else:
            kf_ref[:, (g - n_q) * LANES:(g - n_q + 1) * LANES] = y
            kb_ref[:, (g - n_q) * LANES:(g - n_q + 1) * LANES] = y.astype(BF16)
    vb_ref[...] = x_ref[:, (n_q + n_k) * LANES:].astype(BF16)


def rope_split(qkv, cos2, sin2, n_qk_ch, n_v_ch, q_scale):
    M = qkv.shape[0]
    Lc = cos2.shape[0]
    tr = min(M, 256)
    nb = Lc // tr
    n_q = n_qk_ch // LANES
    return pl.pallas_call(
        functools.partial(_rope_kernel, n_q=n_q, n_k=n_q, q_scale=q_scale),
        grid=(M // tr,),
        in_specs=[pl.BlockSpec((tr, qkv.shape[1]), lambda i: (i, 0)),
                  pl.BlockSpec((tr, LANES), lambda i: (i % nb, 0)),
                  pl.BlockSpec((tr, LANES), lambda i: (i % nb, 0))],
        out_specs=[pl.BlockSpec((tr, n_qk_ch), lambda i: (i, 0)),
                   pl.BlockSpec((tr, n_qk_ch), lambda i: (i, 0)),
                   pl.BlockSpec((tr, n_qk_ch), lambda i: (i, 0)),
                   pl.BlockSpec((tr, n_v_ch), lambda i: (i, 0))],
        out_shape=[jax.ShapeDtypeStruct((M, n_qk_ch), BF16),
                   jax.ShapeDtypeStruct((M, n_qk_ch), F32),
                   jax.ShapeDtypeStruct((M, n_qk_ch), BF16),
                   jax.ShapeDtypeStruct((M, n_v_ch), BF16)],
        compiler_params=_cparams(("parallel",), 40),
        name="rope_split",
    )(qkv, cos2, sin2)


def _pack_kv_kernel(*refs, n, HC, H, TR):
    k_refs, q_refs = refs[:n], refs[n:2 * n]
    ko_ref, vo_ref = refs[2 * n:]
    layer = pl.program_id(0)
    W = 2 * LANES
    for idx in range(n):
        @pl.when(layer == idx)
        def _():
            for hc in range(HC):
                ko_ref[pl.ds(hc, TR, stride=HC), :] = k_refs[idx][:, hc * LANES:(hc + 1) * LANES]
            for h in range(H):
                vo_ref[:, h, :] = q_refs[idx][:, h * W:(h + 1) * W]


def pack_kv(kfs, qkvs, H):
    n = len(kfs)
    M = kfs[0].shape[0]
    TR = 256
    nb = M // TR
    W = 2 * LANES
    HC = 2 * H
    v_blk = (qkvs[0].shape[1] - H * W) // (H * W)

    def frozen(idx, col):
        return lambda l, i: (jnp.where(l == idx, i, jnp.where(l > idx, nb - 1, 0)), col)

    return pl.pallas_call(
        functools.partial(_pack_kv_kernel, n=n, HC=HC, H=H, TR=TR),
        grid=(n, nb),
        in_specs=[pl.BlockSpec((TR, H * W), frozen(idx, 0)) for idx in range(n)]
        + [pl.BlockSpec((TR, H * W), frozen(idx, v_blk)) for idx in range(n)],
        out_specs=[pl.BlockSpec((None, TR * HC, LANES), lambda l, i: (l, i, 0)),
                   pl.BlockSpec((None, TR, H, W), lambda l, i: (l, i, 0, 0))],
        out_shape=[jax.ShapeDtypeStruct((n, M * HC, LANES), F32), jax.ShapeDtypeStruct((n, M, H, W), F32)],
        compiler_params=_cparams(("arbitrary", "arbitrary"), 32),
        name="pack_kv",
    )(*kfs, *qkvs)


def _rope_tables(pos):
    half = LANES // 2
    inv = 1.0 / (ROPE_THETA ** (jnp.arange(half, dtype=F32) * (2.0 / LANES)))
    ang = pos.astype(F32)[:, None] * inv[None, :]
    cos, sin = jnp.cos(ang), jnp.sin(ang)
    return jnp.concatenate([cos, cos], axis=-1), jnp.concatenate([-sin, sin], axis=-1)


def _diff_lambda_in_kernel(lam_ref, lam_init):
    lf = lam_ref[...]
    s1 = jnp.sum(lf[0:1, :] * lf[1:2, :], axis=-1, keepdims=True)
    s2 = jnp.sum(lf[2:3, :] * lf[3:4, :], axis=-1, keepdims=True)
    return jnp.exp(s1) - jnp.exp(s2) + lam_init


def _subln(o, g_ref, lam_init, eps=1e-5):
    y = o * lax.rsqrt(jnp.mean(o * o, axis=-1, keepdims=True) + eps)
    return (y * g_ref[...]) * (1.0 - lam_init)


def _flash_kernel(q_ref, k_ref, v_ref, lam_ref, g_ref, o_ref, m_scr, l_scr, acc_scr, *, T, lam_init):
    qi = pl.program_id(2)
    DH = LANES
    m_scr[...] = jnp.full(m_scr.shape, -jnp.inf, F32)
    l_scr[...] = jnp.zeros(l_scr.shape, F32)
    acc_scr[...] = jnp.zeros(acc_scr.shape, F32)
    q = q_ref[...]
    n_lt = T // LANES

    def tile(kt, masked):
        r0 = pl.multiple_of(kt * T, T)
        k = k_ref[pl.ds(r0, T), :]
        v = v_ref[pl.ds(r0, T), :]
        for c in range(2):
            s = lax.dot_general(q[:, c * DH:(c + 1) * DH], k[:, c * DH:(c + 1) * DH],
                                (((1,), (1,)), ((), ())), preferred_element_type=F32)
            if masked:
                rr = lax.broadcasted_iota(jnp.int32, (T, T), 0)
                cc = lax.broadcasted_iota(jnp.int32, (T, T), 1)
                s = jnp.where(cc <= rr, s, -jnp.inf)
            m_old = m_scr[c]
            m_new = jnp.maximum(m_old, jnp.max(s, axis=-1, keepdims=True))
            corr = jnp.exp2(m_old - m_new)
            e = [jnp.exp2(s[:, j * LANES:(j + 1) * LANES] - m_new) for j in range(n_lt)]
            rs = e[0]
            for j in range(1, n_lt):
                rs = rs + e[j]
            l_scr[c] = l_scr[c] * corr + jnp.sum(rs, axis=-1, keepdims=True)
            eb = jnp.concatenate([x.astype(BF16) for x in e], axis=1)
            pv = jnp.dot(eb, v, preferred_element_type=F32)
            acc_scr[c] = acc_scr[c] * jnp.concatenate([corr, corr], axis=1) + pv
            m_scr[c] = m_new

    def body(kt, carry):
        tile(kt, False)
        return carry

    lax.fori_loop(0, qi, body, 0)
    tile(qi, True)
    lam = _diff_lambda_in_kernel(lam_ref, lam_init)
    l0 = jnp.concatenate([l_scr[0], l_scr[0]], axis=1)
    l1 = jnp.concatenate([l_scr[1], l_scr[1]], axis=1)
    o = acc_scr[0] / l0 - lam * (acc_scr[1] / l1)
    o_ref[...] = _subln(o, g_ref, lam_init).astype(o_ref.dtype)


def flash_diff_attention(qb, kb, vb, lam_rows, subln, B, L, H, lam_init):
    T = 512
    W = 2 * LANES
    nq = L // T
    return pl.pallas_call(
        functools.partial(_flash_kernel, T=T, lam_init=lam_init),
        grid=(B, H, nq),
        in_specs=[pl.BlockSpec((T, W), lambda b, h, i: (b * nq + i, h)),
                  pl.BlockSpec((L, W), lambda b, h, i: (b, h)),
                  pl.BlockSpec((L, W), lambda b, h, i: (b, h)),
                  pl.BlockSpec((4, LANES), lambda b, h, i: (0, 0)),
                  pl.BlockSpec((1, W), lambda b, h, i: (0, 0))],
        out_specs=pl.BlockSpec((T, W), lambda b, h, i: (b * nq + i, h)),
        out_shape=jax.ShapeDtypeStruct((B * L, H * W), BF16),
        scratch_shapes=[pltpu.VMEM((2, T, LANES), F32), pltpu.VMEM((2, T, LANES), F32), pltpu.VMEM((2, T, W), F32)],
        compiler_params=_cparams(("parallel", "parallel", "arbitrary"), 40),
        name="flash_diff_attention",
    )(qb, kb, vb, lam_rows, subln.reshape(1, W))


def _paged_kernel(pt_ref, q_ref, kn_ref, vn_ref, lam_ref, g_ref, *rest, P, H, T, TP, n_steps, lam_init):
    k_refs = rest[:P]
    v_refs = rest[P:2 * P]
    o_ref = rest[2 * P]
    m_scr, l_scr, acc_scr = rest[2 * P + 1:]
    p = pl.program_id(1)
    NR = H * T
    q = q_ref[0]
    nt = functools.partial(lax.dot_general, dimension_numbers=(((1,), (1,)), ((), ())),
                           preferred_element_type=F32)

    def masked(s, causal):
        row = lax.broadcasted_iota(jnp.int32, s.shape, 0) % NR
        col = lax.broadcasted_iota(jnp.int32, s.shape, 1)
        ok = col % H == row // T
        if causal:
            ok = jnp.logical_and(ok, col // H <= row % T)
        return jnp.where(ok, s, -jnp.inf)

    def update(s, v):
        m_old = m_scr[...]
        m_new = jnp.maximum(m_old, jnp.max(s, axis=-1, keepdims=True))
        corr = jnp.exp(m_old - m_new)
        e = jnp.exp(s - m_new)
        l_scr[...] = l_scr[...] * corr + jnp.sum(e, axis=-1, keepdims=True)
        m_scr[...] = m_new
        acc_scr[...] = acc_scr[...] * corr + jnp.dot(e.astype(BF16), v, preferred_element_type=F32)

    @pl.when(p == 0)
    def _():
        m_scr[...] = jnp.full(m_scr.shape, -jnp.inf, F32)
        l_scr[...] = jnp.zeros(l_scr.shape, F32)
        acc_scr[...] = jnp.zeros(acc_scr.shape, F32)
        kn = kn_ref[0]
        s = jnp.concatenate([nt(q[c], kn[c]) for c in range(2)], axis=0)
        update(masked(s, True), vn_ref[0])

    ss = []
    for c in range(2):
        kc = jnp.concatenate([k_refs[r][pl.ds(c, TP * H, stride=2), :] for r in range(P)], axis=0)
        ss.append(nt(q[c], kc.astype(BF16)))
    v = jnp.concatenate([v_refs[r][...] for r in range(P)], axis=0).astype(BF16)
    update(masked(jnp.concatenate(ss, axis=0), False), v)

    @pl.when(p == n_steps - 1)
    def _():
        lam = _diff_lambda_in_kernel(lam_ref, lam_init)
        po = acc_scr[...] / l_scr[...]
        o = po[:NR, :] - lam * po[NR:, :]
        o_ref[0] = _subln(o, g_ref, lam_init).astype(o_ref.dtype)


def paged_diff_attention(qs, kn, vn, k_pool, v_pool, layer, page_table, lam_rows, subln, H, T, lam_init):
    B = qs.shape[0]
    n_pages = page_table.shape[1]
    TP = k_pool.shape[2] // (2 * H)
    P = 4
    n_steps = n_pages // P
    W = 2 * LANES
    NR = H * T

    def kmap(r):
        return lambda b, p, pt: (layer, pt[b, p * P + r], 0, 0)

    in_specs = [pl.BlockSpec((1, 2, NR, LANES), lambda b, p, pt: (b, 0, 0, 0)),
                pl.BlockSpec((1, 2, NR, LANES), lambda b, p, pt: (b, 0, 0, 0)),
                pl.BlockSpec((1, NR, W), lambda b, p, pt: (b, 0, 0)),
                pl.BlockSpec((4, LANES), lambda b, p, pt: (0, 0)),
                pl.BlockSpec((1, W), lambda b, p, pt: (0, 0))]
    in_specs += [pl.BlockSpec((None, None, TP * 2 * H, LANES), kmap(r)) for r in range(P)]
    in_specs += [pl.BlockSpec((None, None, TP * H, W), kmap(r)) for r in range(P)]
    return pl.pallas_call(
        functools.partial(_paged_kernel, P=P, H=H, T=T, TP=TP, n_steps=n_steps, lam_init=lam_init),
        grid_spec=pltpu.PrefetchScalarGridSpec(
            num_scalar_prefetch=1,
            grid=(B, n_steps),
            in_specs=in_specs,
            out_specs=pl.BlockSpec((1, NR, W), lambda b, p, pt: (b, 0, 0)),
            scratch_shapes=[pltpu.VMEM((2 * NR, 1), F32), pltpu.VMEM((2 * NR, 1), F32),
                            pltpu.VMEM((2 * NR, W), F32)],
        ),
        out_shape=jax.ShapeDtypeStruct((B, NR, W), BF16),
        compiler_params=_cparams(("parallel", "arbitrary"), 48),
        name="paged_diff_attention",
    )(page_table, qs, kn, vn, lam_rows, subln.reshape(1, W), *([k_pool] * P), *([v_pool] * P))


def _mem_attn_kernel(q_ref, k_ref, v_ref, o_ref, *, H, scale):
    q = q_ref[0]
    k = k_ref[0].astype(BF16)
    v = v_ref[0].astype(BF16)
    for h in range(H):
        sl = slice(h * LANES, (h + 1) * LANES)
        s = lax.dot_general(q[:, sl], k[:, sl], (((1,), (1,)), ((), ())), preferred_element_type=F32) * scale
        m = jnp.max(s, axis=-1, keepdims=True)
        e = jnp.exp(s - m)
        p = e / jnp.sum(e, axis=-1, keepdims=True)
        o_ref[0, :, sl] = jnp.dot(p.astype(BF16), v[:, sl], preferred_element_type=F32).astype(o_ref.dtype)


def mem_attention(q, mk, mv, H):
    B, L, D = q.shape
    Mm = mk.shape[1]
    tq = min(L, 512)
    return pl.pallas_call(
        functools.partial(_mem_attn_kernel, H=H, scale=LANES ** -0.5),
        grid=(B, L // tq),
        in_specs=[pl.BlockSpec((1, tq, D), lambda b, i: (b, i, 0)),
                  pl.BlockSpec((1, Mm, D), lambda b, i: (b, 0, 0)),
                  pl.BlockSpec((1, Mm, D), lambda b, i: (b, 0, 0))],
        out_specs=pl.BlockSpec((1, tq, D), lambda b, i: (b, i, 0)),
        out_shape=jax.ShapeDtypeStruct((B, L, D), BF16),
        compiler_params=_cparams(("parallel", "parallel"), 24),
        name="mem_attention",
    )(q, mk, mv)


def _gdn_mixer(hn, B, L, conv_state, s0, j, w_in, conv_w, a_log, dt_bias, norm_g, w_out, x_res, dims):
    HK, HV, n_qk, n_conv = dims
    h = matmul(hn, w_in, j, n_out=n_conv + HV * LANES, w_is_t=True)
    ba = matmul_tail(hn, w_in, j, n_conv + HV * LANES)
    q_scale = LANES ** -0.5
    if conv_state is None:
        qkv = conv_prompt(h, conv_w, B, L, n_qk, n_conv, q_scale)
        new_conv = h.reshape(B, L, -1)[:, L - (CONV_W - 1):, :n_conv]
        NC = L // CHUNK
        zsrc, z_col0 = h, n_conv
        beta, gc, gct = gdn_gates(ba, a_log, dt_bias, CHUNK, HV)
    else:
        u = h[:, :n_conv].reshape(B, L, n_conv)
        ut = jnp.swapaxes(u, 0, 1)
        st = jnp.swapaxes(conv_state, 0, 1)
        y = conv_sample(ut, st, conv_w, n_qk, q_scale)
        new_conv = jnp.concatenate([conv_state, u], axis=1)[:, L:, :]
        padrows = lambda t: jnp.pad(t.reshape(B, L, -1), ((0, 0), (0, CHUNK - L), (0, 0))).reshape(B * CHUNK, -1)
        qkv = padrows(jnp.swapaxes(y, 0, 1))
        zsrc, z_col0 = padrows(h[:, n_conv:]), 0
        NC = 1
        beta, gc, gct = gdn_gates(padrows(ba), a_log, dt_bias, L, HV)
    og, s_new = gdn_recurrence(qkv, zsrc, z_col0, beta, gc, gct, s0, j if s0.shape[0] > 1 else 0, norm_g,
                               B, NC, HK, HV)
    if conv_state is not None:
        og = og.reshape(B, CHUNK, -1)[:, :L].reshape(B * L, -1)
    return matmul(og, w_out, j, res=x_res), new_conv, s_new


def _forward(x, mem_kv_fn, B, L, pos, gdn_state, attn_fn, attn_q_scale, params):
    (norm_mix, gdn_w_in, gdn_conv_w, gdn_a_log, gdn_dt_bias, gdn_norm, gdn_w_out, attn_w_qkv,
     attn_lambda, attn_subln, attn_w_o, norm_xattn, w_xq, w_xo, norm_ffn, w_up, w_down, norm_final) = params
    depth = norm_mix.shape[0]
    D = x.shape[1]
    HV = gdn_a_log.shape[1]
    HK = HV // 2
    n_qk = HK * LANES
    n_conv = gdn_conv_w.shape[2]
    XH = w_xq.shape[2] // LANES
    cos2, sin2 = _rope_tables(pos)
    if L < 256:
        cos2, sin2 = jnp.tile(cos2, (B, 1)), jnp.tile(sin2, (B, 1))
    Lq = max(L, SLOT_ROWS)
    convs, deltas, ks, vs = [], [], [], []
    for i in range(depth):
        j = i // 2
        hn = rmsnorm(x, norm_mix[i], BF16)
        if i % 2 == 0:
            conv_state, s0 = gdn_state(j)
            x, new_conv, s_new = _gdn_mixer(hn, B, L, conv_state, s0, j, gdn_w_in, gdn_conv_w[j],
                                            gdn_a_log[j], gdn_dt_bias[j], gdn_norm[j], gdn_w_out, x,
                                            (HK, HV, n_qk, n_conv))
            convs.append(new_conv)
            deltas.append(s_new)
        else:
            lam_init = 0.8 - 0.6 * math.exp(-0.3 * i)
            n_qk_a = attn_w_qkv.shape[2] // 3
            qkv = matmul(hn, attn_w_qkv, j)
            qb, kf, kb, vb = rope_split(qkv, cos2, sin2, n_qk_a, n_qk_a, attn_q_scale)
            oa = attn_fn(j, qb, kb, vb, attn_lambda[j], attn_subln[j], lam_init)
            x = matmul(oa, attn_w_o, j, res=x)
            ks.append(kf)
            vs.append(qkv)
        hx = rmsnorm(x, norm_xattn[i], BF16)
        q = matmul(hx, w_xq, i, out_dtype=BF16).reshape(B, L, -1)
        mk, mv = mem_kv_fn(i)
        om = mem_attention(jnp.pad(q, ((0, 0), (0, Lq - L), (0, 0))), mk, mv, XH)[:, :L]
        x = matmul(om.reshape(B * L, -1), w_xo, i, res=x)
        hf = rmsnorm(x, norm_ffn[i], BF16)
        up = matmul(hf, w_up, i, out_dtype=BF16, act="relu2")
        x = matmul(up, w_down, i, res=x)
    y = rmsnorm(x, norm_final, F32)
    return y, convs, deltas, ks, vs


def kernel(x_prompt, x_sample, mem_prompt, state_conv, state_delta, cache_k, cache_v, cache_mem_k, cache_mem_v, page_table, norm_mix, gdn_w_in, gdn_conv_w, gdn_a_log, gdn_dt_bias, gdn_norm, gdn_w_out, attn_w_qkv, attn_lambda, attn_subln, attn_w_o, norm_xattn, norm_mem, w_xq, w_xkv, w_xo, norm_ffn, w_up, w_down, norm_final):
    Bp, Lp, D = x_prompt.shape
    Bs, Ls, _ = x_sample.shape
    depth = norm_mix.shape[0]
    HV = gdn_a_log.shape[1]
    n_conv = gdn_conv_w.shape[2]
    n_attn, n_pool, TP, H = cache_k.shape[:4]
    W = 2 * LANES
    XH = w_xq.shape[2] // LANES
    Mm = mem_prompt.shape[1]
    past_len = page_table.shape[1] * TP

    gdn_w_in_t = jnp.swapaxes(gdn_w_in, 1, 2)
    params = (norm_mix, gdn_w_in_t, gdn_conv_w, gdn_a_log, gdn_dt_bias, gdn_norm, gdn_w_out, attn_w_qkv,
              attn_lambda, attn_subln, attn_w_o, norm_xattn, w_xq, w_xo, norm_ffn, w_up, w_down, norm_final)

    p_mk, p_mv = [], []

    def prompt_mem(i):
        mn = rmsnorm(mem_prompt.reshape(Bp * Mm, D), norm_mem[i], BF16)
        kv = matmul(mn, w_xkv, i).reshape(Bp, Mm, 2 * XH * LANES)
        mk, mv = kv[:, :, :XH * LANES], kv[:, :, XH * LANES:]
        p_mk.append(mk.reshape(Bp, Mm, XH, LANES))
        p_mv.append(mv.reshape(Bp, Mm, XH, LANES))
        return mk, mv

    def prompt_attn(j, qb, kb, vb, lam_rows, subln, lam_init):
        return flash_diff_attention(qb, kb, vb, lam_rows, subln, Bp, Lp, H, lam_init)

    zero_state = jnp.zeros((1, Bp, HV, LANES, LANES), F32)
    yp, p_conv, p_delta, p_k, p_v = _forward(
        x_prompt.reshape(Bp * Lp, D), prompt_mem, Bp, Lp, jnp.arange(Lp), lambda j: (None, zero_state),
        prompt_attn, LANES ** -0.5 * math.log2(math.e), params)

    k_pool = cache_k.reshape(n_attn, n_pool, TP * H * 2, LANES)
    v_pool = cache_v.reshape(n_attn, n_pool, TP * H, W)

    def sample_attn(j, qb, kb, vb, lam_rows, subln, lam_init):
        qs = jnp.transpose(qb.reshape(Bs, Ls, H, 2, LANES), (0, 3, 2, 1, 4)).reshape(Bs, 2, H * Ls, LANES)
        kn = jnp.transpose(kb.reshape(Bs, Ls, H, 2, LANES), (0, 3, 1, 2, 4)).reshape(Bs, 2, Ls * H, LANES)
        oa = paged_diff_attention(qs, kn, vb.reshape(Bs, Ls * H, W), k_pool, v_pool, j, page_table,
                                  lam_rows, subln, H, Ls, lam_init)
        return jnp.swapaxes(oa.reshape(Bs, H, Ls, W), 1, 2).reshape(Bs * Ls, H * W)

    def sample_gdn_state(j):
        return state_conv[j], state_delta

    ys, s_conv, s_delta, s_k, s_v = _forward(
        x_sample.reshape(Bs * Ls, D), lambda i: (cache_mem_k[i].reshape(Bs, Mm, -1), cache_mem_v[i].reshape(Bs, Mm, -1)),
        Bs, Ls, past_len + jnp.arange(Ls), sample_gdn_state, sample_attn, LANES ** -0.5, params)

    pk_all, pv_all = pack_kv(p_k, p_v, H)
    s_v = [t[:, t.shape[1] - H * W:] for t in s_v]
    return (yp.reshape(Bp, Lp, D), ys.reshape(Bs, Ls, D),
            jnp.stack(p_conv), jnp.stack(p_delta), jnp.stack(s_conv), jnp.stack(s_delta),
            pk_all.reshape(n_attn, Bp, Lp, H, 2, LANES), pv_all.reshape(n_attn, Bp, Lp, H, W),
            jnp.stack(s_k).reshape(n_attn, Bs, Ls, H, 2, LANES), jnp.stack(s_v).reshape(n_attn, Bs, Ls, H, W),
            jnp.stack(p_mk), jnp.stack(p_mv))
```

```python
import functools
import math

import jax
import jax.numpy as jnp
from jax import lax
from jax.experimental import pallas as pl
from jax.experimental.pallas import tpu as pltpu

F32 = jnp.float32
BF16 = jnp.bfloat16

LANES = 128
CHUNK = 64
CONV_W = 4
EPS = 1e-6
ROPE_THETA = 10000.0
VMEM_CAP = 60 * 1024 * 1024
SLOT_ROWS = 16


def _cparams(sem, vmem_mb):
    return pltpu.CompilerParams(dimension_semantics=sem, vmem_limit_bytes=min(VMEM_CAP, vmem_mb * 1024 * 1024))


def _rmsnorm_kernel(x_ref, g_ref, o_ref, *, eps):
    x = x_ref[...]
    y = x * lax.rsqrt(jnp.mean(x * x, axis=-1, keepdims=True) + eps)
    o_ref[...] = (y * g_ref[...]).astype(o_ref.dtype)


def rmsnorm(x, gain, out_dtype, eps=EPS):
    M, D = x.shape
    tm = min(M, 512)
    return pl.pallas_call(
        functools.partial(_rmsnorm_kernel, eps=eps),
        grid=(M // tm,),
        in_specs=[pl.BlockSpec((tm, D), lambda i: (i, 0)), pl.BlockSpec((1, D), lambda i: (0, 0))],
        out_specs=pl.BlockSpec((tm, D), lambda i: (i, 0)),
        out_shape=jax.ShapeDtypeStruct((M, D), out_dtype),
        compiler_params=_cparams(("parallel",), 32),
        name="rmsnorm",
    )(x, gain.reshape(1, D))


def _mm_kernel(*refs, nk, act, has_res, w_is_t):
    it = iter(refs)
    a_ref = next(it)
    w_ref = next(it)
    r_ref = next(it) if has_res else None
    o_ref = next(it)
    acc_ref = next(it) if nk > 1 else None
    contract_w = 1 if w_is_t else 0
    p = lax.dot_general(a_ref[...].astype(BF16), w_ref[...].astype(BF16), (((1,), (contract_w,)), ((), ())),
                        preferred_element_type=F32)

    def finish(acc):
        if act == "relu2":
            r = jnp.maximum(acc, 0.0)
            acc = r * r
        if has_res:
            acc = acc + r_ref[...]
        o_ref[...] = acc.astype(o_ref.dtype)

    if nk == 1:
        finish(p)
    else:
        k = pl.program_id(2)

        @pl.when(k == 0)
        def _():
            acc_ref[...] = p

        @pl.when(k > 0)
        def _():
            acc_ref[...] += p

        @pl.when(k == nk - 1)
        def _():
            finish(acc_ref[...])


def matmul(a, w, layer, *, n_out=None, out_dtype=F32, act=None, res=None, w_is_t=False):
    M, K = a.shape
    N = n_out if n_out is not None else w.shape[1 if w_is_t else 2]
    if M >= 2048 and K <= 2048:
        tm, tn, tk = 2048, min(N, 512), K
    elif M >= 2048 and K <= 4096:
        tm, tn, tk = 1024, min(N, 512), K
    elif M >= 2048:
        tm, tn, tk = 2048, min(N, 512), 2048
    else:
        tm, tn, tk = M, min(N, 1024), min(K, 2048)
    nk = K // tk
    if w_is_t:
        w_spec = pl.BlockSpec((None, tn, tk), lambda i, j, k: (layer, j, k))
    else:
        w_spec = pl.BlockSpec((None, tk, tn), lambda i, j, k: (layer, k, j))
    in_specs = [pl.BlockSpec((tm, tk), lambda i, j, k: (i, k)), w_spec]
    args = [a, w]
    if res is not None:
        in_specs.append(pl.BlockSpec((tm, tn), lambda i, j, k: (i, j)))
        args.append(res)
    out_b = jnp.dtype(out_dtype).itemsize
    est = 2 * (tm * tk * a.dtype.itemsize + tk * tn * 4 + tm * tn * out_b + (tm * tn * 4 if res is not None else 0))
    est += tm * tn * 4 * (2 if nk > 1 else 1) + tk * tn * 2
    return pl.pallas_call(
        functools.partial(_mm_kernel, nk=nk, act=act, has_res=res is not None, w_is_t=w_is_t),
        grid=(M // tm, N // tn, nk),
        in_specs=in_specs,
        out_specs=pl.BlockSpec((tm, tn), lambda i, j, k: (i, j)),
        out_shape=jax.ShapeDtypeStruct((M, N), out_dtype),
        scratch_shapes=[pltpu.VMEM((tm, tn), F32)] if nk > 1 else [],
        compiler_params=_cparams(("parallel", "parallel", "arbitrary"), est // (1024 * 1024) + 8),
        name="matmul",
    )(*args)


def _mm_tail_kernel(a_ref, w_ref, o_ref):
    p = lax.dot_general(a_ref[...], w_ref[...].astype(BF16), (((1,), (1,)), ((), ())), preferred_element_type=F32)
    o_ref[...] = jnp.concatenate([p, jnp.zeros((p.shape[0], LANES - p.shape[1]), F32)], axis=1)


def matmul_tail(a, wt, layer, row0):
    M, K = a.shape
    n_tail = wt.shape[1] - row0
    tm = min(M, 2048)
    return pl.pallas_call(
        _mm_tail_kernel,
        grid=(M // tm,),
        in_specs=[pl.BlockSpec((tm, K), lambda i: (i, 0)),
                  pl.BlockSpec((None, n_tail, K), lambda i: (layer, row0 // n_tail, 0))],
        out_specs=pl.BlockSpec((tm, LANES), lambda i: (i, 0)),
        out_shape=jax.ShapeDtypeStruct((M, LANES), F32),
        compiler_params=_cparams(("parallel",), 32),
        name="matmul_tail",
    )(a, wt)


def _l2_groups(y, scale):
    outs = []
    for g in range(y.shape[-1] // LANES):
        yg = y[:, g * LANES:(g + 1) * LANES]
        ss = jnp.sum(yg * yg, axis=-1, keepdims=True)
        outs.append(yg * (lax.rsqrt(ss + 1e-6) * scale))
    return outs


def _conv_post(y, o_ref, rows, j, n_q, n_k, q_scale):
    y = y * jax.nn.sigmoid(y)

    @pl.when(j < n_q)
    def _():
        for g, yg in enumerate(_l2_groups(y, q_scale)):
            o_ref[rows, g * LANES:(g + 1) * LANES] = yg

    @pl.when(jnp.logical_and(j >= n_q, j < n_q + n_k))
    def _():
        for g, yg in enumerate(_l2_groups(y, 1.0)):
            o_ref[rows, g * LANES:(g + 1) * LANES] = yg

    @pl.when(j >= n_q + n_k)
    def _():
        o_ref[rows, :] = y


def _conv_prompt_kernel(x_ref, w_ref, o_ref, *, L, TR, n_q, n_k, q_scale):
    j = pl.program_id(1)
    w = w_ref[...]

    def run(l2_scale):
        def body(i, carry):
            r0 = pl.multiple_of(i * TR, TR)
            xa = x_ref[pl.ds(r0, TR), :]
            pstart = pl.multiple_of(jnp.maximum(r0 - 8, 0), 8)
            xp = jnp.where(i == 0, 0.0, x_ref[pl.ds(pstart, 8), :])
            xx = jnp.concatenate([xp, xa], axis=0)
            y = xa * w[3:4, :]
            for s in range(1, CONV_W):
                y = y + pltpu.roll(xx, s, axis=0)[8:, :] * w[3 - s:4 - s, :]
            y = y * jax.nn.sigmoid(y)
            if l2_scale is None:
                o_ref[pl.ds(r0, TR), :] = y
            else:
                for g, yg in enumerate(_l2_groups(y, l2_scale)):
                    o_ref[pl.ds(r0, TR), g * LANES:(g + 1) * LANES] = yg
            return carry

        lax.fori_loop(0, L // TR, body, 0)

    pl.when(j < n_q)(lambda: run(q_scale))
    pl.when(jnp.logical_and(j >= n_q, j < n_q + n_k))(lambda: run(1.0))
    pl.when(j >= n_q + n_k)(lambda: run(None))


def conv_prompt(h, conv_w, B, L, n_qk_ch, n_ch, q_scale):
    tc = 256
    n_q = n_qk_ch // tc
    return pl.pallas_call(
        functools.partial(_conv_prompt_kernel, L=L, TR=256, n_q=n_q, n_k=n_q, q_scale=q_scale),
        grid=(B, n_ch // tc),
        in_specs=[pl.BlockSpec((L, tc), lambda b, j: (b, j)), pl.BlockSpec((CONV_W, tc), lambda b, j: (0, j))],
        out_specs=pl.BlockSpec((L, tc), lambda b, j: (b, j)),
        out_shape=jax.ShapeDtypeStruct((B * L, n_ch), F32),
        compiler_params=_cparams(("parallel", "parallel"), 40),
        name="conv_prompt",
    )(h, conv_w)


def _conv_sample_kernel(u_ref, st_ref, w_ref, o_ref, *, T, n_q, n_k, q_scale):
    j = pl.program_id(0)
    w = w_ref[...]
    xc = [st_ref[i] for i in range(CONV_W - 1)] + [u_ref[t] for t in range(T)]
    for t in range(T):
        y = xc[t] * w[0:1, :]
        for i in range(1, CONV_W):
            y = y + xc[t + i] * w[i:i + 1, :]
        _conv_post(y, o_ref.at[t], slice(None), j, n_q, n_k, q_scale)


def conv_sample(u, state, conv_w, n_qk_ch, q_scale):
    T, B, C = u.shape
    tc = 1024
    n_q = n_qk_ch // tc
    return pl.pallas_call(
        functools.partial(_conv_sample_kernel, T=T, n_q=n_q, n_k=n_q, q_scale=q_scale),
        grid=(C // tc,),
        in_specs=[pl.BlockSpec((T, B, tc), lambda j: (0, 0, j)),
                  pl.BlockSpec((CONV_W - 1, B, tc), lambda j: (0, 0, j)),
                  pl.BlockSpec((CONV_W, tc), lambda j: (0, j))],
        out_specs=pl.BlockSpec((T, B, tc), lambda j: (0, 0, j)),
        out_shape=jax.ShapeDtypeStruct((T, B, C), F32),
        compiler_params=_cparams(("parallel",), 16),
        name="conv_sample",
    )(u, state, conv_w)


def _gates_kernel(ba_ref, alog_ref, dtb_ref, beta_ref, gc_ref, gct_ref, *, valid, n_heads):
    b = ba_ref[...]
    a = pltpu.roll(b, LANES - n_heads, axis=1)
    tb = b.shape[0]
    rc = lax.broadcasted_iota(jnp.int32, (tb, LANES), 0) % CHUNK
    ok = rc < valid
    beta = jnp.where(ok, jax.nn.sigmoid(b), 0.0)
    z = a + dtb_ref[...]
    softplus = jnp.maximum(z, 0.0) + jnp.log1p(jnp.exp(-jnp.abs(z)))
    g = jnp.where(ok, -jnp.exp(alog_ref[...]) * softplus, 0.0)
    s = 1
    while s < CHUNK:
        g = g + jnp.where(rc >= s, pltpu.roll(g, s, axis=0), 0.0)
        s *= 2
    beta_ref[...] = beta
    gc_ref[...] = g
    g3 = g.reshape(tb // CHUNK, CHUNK, LANES)
    gz = jnp.concatenate([g3, jnp.zeros_like(g3)], axis=1).reshape(2 * tb, LANES)
    gct_ref[...] = gz.T[:n_heads, :]


def gdn_gates(ba, a_log, dt_bias, valid, n_heads):
    M = ba.shape[0]
    tb = min(M, 512)
    pad = lambda v: jnp.pad(v.astype(F32), (0, LANES - n_heads)).reshape(1, LANES)
    return pl.pallas_call(
        functools.partial(_gates_kernel, valid=valid, n_heads=n_heads),
        grid=(M // tb,),
        in_specs=[pl.BlockSpec((tb, LANES), lambda i: (i, 0)),
                  pl.BlockSpec((1, LANES), lambda i: (0, 0)),
                  pl.BlockSpec((1, LANES), lambda i: (0, 0))],
        out_specs=[pl.BlockSpec((tb, LANES), lambda i: (i, 0)),
                   pl.BlockSpec((tb, LANES), lambda i: (i, 0)),
                   pl.BlockSpec((n_heads, 2 * tb), lambda i: (0, i))],
        out_shape=[jax.ShapeDtypeStruct((M, LANES), F32),
                   jax.ShapeDtypeStruct((M, LANES), F32),
                   jax.ShapeDtypeStruct((n_heads, 2 * M), F32)],
        compiler_params=_cparams(("parallel",), 16),
        name="gdn_gates",
    )(ba, pad(a_log), pad(dt_bias))


def _bdot(a, b):
    return jnp.dot(a.astype(BF16), b.astype(BF16), preferred_element_type=F32)


def _bdot_nt(a, b):
    return lax.dot_general(a.astype(BF16), b.astype(BF16), (((1,), (1,)), ((), ())), preferred_element_type=F32)


def _bdot_tn(a, b):
    return lax.dot_general(a.astype(BF16), b.astype(BF16), (((0,), (0,)), ((), ())), preferred_element_type=F32)


def _gdn_kernel(q_ref, k_ref, v_ref, z_ref, beta_ref, gc_ref, gr_ref, s0_ref, ng_ref, o_ref, so_ref, s_scr,
                *, G, NC, eps):
    hg = pl.program_id(1)
    n = pl.program_id(2)
    C = CHUNK

    @pl.when(n == 0)
    def _():
        s_scr[...] = s0_ref[0]

    row = lax.broadcasted_iota(jnp.int32, (C, C), 0)
    col = lax.broadcasted_iota(jnp.int32, (C, C), 1)
    incl = row >= col
    strict = row > col
    eye = (row == col).astype(F32)
    lane = lax.broadcasted_iota(jnp.int32, (C, LANES), 1)
    beta_all = beta_ref[...]
    gc_all = gc_ref[...]
    ng = ng_ref[...]
    heads = range(G)
    dot = functools.partial(jnp.dot, preferred_element_type=F32)

    qs = [q_ref[:, kh * LANES:(kh + 1) * LANES] for kh in range(G // 2)]
    ks = [k_ref[:, kh * LANES:(kh + 1) * LANES] for kh in range(G // 2)]
    qkk = [_bdot_nt(jnp.concatenate([qs[kh], ks[kh]], axis=0), ks[kh]) for kh in range(G // 2)]

    beta, eg, et, egl, dec, a, t = [], [], [], [], [], [], []
    for g in heads:
        sel = lane == hg * G + g
        b_g = jnp.sum(jnp.where(sel, beta_all, 0.0), axis=-1, keepdims=True)
        gc = jnp.sum(jnp.where(sel, gc_all, 0.0), axis=-1, keepdims=True)
        gl = gc[C - 1:C, :]
        gr = gr_ref[g:g + 1, :][:, :C]
        d_g = jnp.exp(jnp.where(incl, gc - gr, -jnp.inf))
        a_g = b_g * qkk[g // 2][C:] * jnp.where(strict, d_g, 0.0)
        beta.append(b_g)
        eg.append(jnp.exp(gc))
        et.append(jnp.exp(gl - gc))
        egl.append(jnp.exp(gl))
        dec.append(d_g)
        a.append(a_g)
        t.append(eye - jnp.where(row // 2 == col // 2, a_g, 0.0))

    s = 2
    while s < C:
        fmask = jnp.logical_and(row // (2 * s) == col // (2 * s), row // s != col // s)
        tb = [t[g].astype(BF16) for g in heads]
        x = [dot(tb[g], jnp.where(fmask, a[g], 0.0).astype(BF16)) for g in heads]
        t = [t[g] - dot(x[g].astype(BF16), tb[g]) for g in heads]
        s *= 2

    sol = []
    for g in heads:
        v = v_ref[:, g * LANES:(g + 1) * LANES]
        rhs = jnp.concatenate([beta[g] * v, (beta[g] * eg[g]) * ks[g // 2]], axis=1)
        sol.append(_bdot(t[g], rhs))
    S = [s_scr[g] for g in heads]
    wq = [_bdot(jnp.concatenate([sol[g][:, LANES:], qs[g // 2]], axis=0), S[g]) for g in heads]
    u = [sol[g][:, :LANES] - wq[g][:C] for g in heads]
    o1 = [_bdot(qkk[g // 2][:C] * dec[g], u[g]) for g in heads]
    for g in heads:
        s_scr[g] = egl[g] * S[g] + _bdot_tn(ks[g // 2] * et[g], u[g])
    for g in heads:
        o = eg[g] * wq[g][C:] + o1[g]
        on = o * lax.rsqrt(jnp.mean(o * o, axis=-1, keepdims=True) + eps) * ng
        zz = z_ref[:, g * LANES:(g + 1) * LANES]
        o_ref[:, g * LANES:(g + 1) * LANES] = (on * (zz * jax.nn.sigmoid(zz))).astype(o_ref.dtype)

    @pl.when(n == NC - 1)
    def _():
        so_ref[0] = s_scr[...]


def gdn_recurrence(qkv, zsrc, z_col0, beta, gc, gct, s0, layer, norm_g, B, NC, HK, HV, G=32):
    rows = B * NC * CHUNK
    GK = G // 2
    n_g = HV // G
    kq_w = GK * LANES
    v_w = G * LANES
    k_blk0 = (HK * LANES) // kq_w
    v_blk0 = (2 * HK * LANES) // v_w
    z_blk0 = z_col0 // v_w
    rmap = lambda b, g, n: b * NC + n
    return pl.pallas_call(
        functools.partial(_gdn_kernel, G=G, NC=NC, eps=EPS),
        grid=(B, n_g, NC),
        in_specs=[
            pl.BlockSpec((CHUNK, kq_w), lambda b, g, n: (rmap(b, g, n), g)),
            pl.BlockSpec((CHUNK, kq_w), lambda b, g, n: (rmap(b, g, n), k_blk0 + g)),
            pl.BlockSpec((CHUNK, v_w), lambda b, g, n: (rmap(b, g, n), v_blk0 + g)),
            pl.BlockSpec((CHUNK, v_w), lambda b, g, n: (rmap(b, g, n), z_blk0 + g)),
            pl.BlockSpec((CHUNK, LANES), lambda b, g, n: (rmap(b, g, n), 0)),
            pl.BlockSpec((CHUNK, LANES), lambda b, g, n: (rmap(b, g, n), 0)),
            pl.BlockSpec((G, LANES), lambda b, g, n: (g, rmap(b, g, n))),
            pl.BlockSpec((None, 1, G, LANES, LANES), lambda b, g, n: (layer, b, g, 0, 0)),
            pl.BlockSpec((1, LANES), lambda b, g, n: (0, 0)),
        ],
        out_specs=[
            pl.BlockSpec((CHUNK, v_w), lambda b, g, n: (rmap(b, g, n), g)),
            pl.BlockSpec((1, G, LANES, LANES), lambda b, g, n: (b, g, 0, 0)),
        ],
        out_shape=[jax.ShapeDtypeStruct((rows, HV * LANES), BF16),
                   jax.ShapeDtypeStruct((B, HV, LANES, LANES), F32)],
        scratch_shapes=[pltpu.VMEM((G, LANES, LANES), F32)],
        compiler_params=_cparams(("parallel", "parallel", "arbitrary"), 32),
        name="gdn_recurrence",
    )(qkv, qkv, qkv, zsrc, beta, gc, gct, s0, norm_g.reshape(1, LANES))


def _rope_kernel(x_ref, cos_ref, sin_ref, qb_ref, kf_ref, kb_ref, vb_ref, *, n_q, n_k, q_scale):
    cos = cos_ref[...]
    sin = sin_ref[...]
    for g in range(n_q + n_k):
        x = x_ref[:, g * LANES:(g + 1) * LANES]
        y = x * cos + pltpu.roll(x, LANES // 2, axis=1) * sin
        if g < n_q:
            if q_scale != 1.0:
                y = y * q_scale
            qb_ref[:, g * LANES:(g + 1) * LANES] = y.astype(BF16)
        else:
            kf_ref[:, (g - n_q) * LANES:(g - n_q + 1) * LANES] = y
            kb_ref[:, (g - n_q) * LANES:(g - n_q + 1) * LANES] = y.astype(BF16)
    vb_ref[...] = x_ref[:, (n_q + n_k) * LANES:].astype(BF16)


def rope_split(qkv, cos2, sin2, n_qk_ch, n_v_ch, q_scale):
    M = qkv.shape[0]
    Lc = cos2.shape[0]
    tr = min(M, 256)
    nb = Lc // tr
    n_q = n_qk_ch // LANES
    return pl.pallas_call(
        functools.partial(_rope_kernel, n_q=n_q, n_k=n_q, q_scale=q_scale),
        grid=(M // tr,),
        in_specs=[pl.BlockSpec((tr, qkv.shape[1]), lambda i: (i, 0)),
                  pl.BlockSpec((tr, LANES), lambda i: (i % nb, 0)),
                  pl.BlockSpec((tr, LANES), lambda i: (i % nb, 0))],
        out_specs=[pl.BlockSpec((tr, n_qk_ch), lambda i: (i, 0)),
                   pl.BlockSpec((tr, n_qk_ch), lambda i: (i, 0)),
                   pl.BlockSpec((tr, n_qk_ch), lambda i: (i, 0)),
                   pl.BlockSpec((tr, n_v_ch), lambda i: (i, 0))],
        out_shape=[jax.ShapeDtypeStruct((M, n_qk_ch), BF16),
                   jax.ShapeDtypeStruct((M, n_qk_ch), F32),
                   jax.ShapeDtypeStruct((M, n_qk_ch), BF16),
                   jax.ShapeDtypeStruct((M, n_v_ch), BF16)],
        compiler_params=_cparams(("parallel",), 40),
        name="rope_split",
    )(qkv, cos2, sin2)


def _pack_kv_kernel(*refs, n, HC, H, TR):
    k_refs, q_refs = refs[:n], refs[n:2 * n]
    ko_ref, vo_ref = refs[2 * n:]
    layer = pl.program_id(0)
    W = 2 * LANES
    for idx in range(n):
        @pl.when(layer == idx)
        def _():
            for hc in range(HC):
                ko_ref[pl.ds(hc, TR, stride=HC), :] = k_refs[idx][:, hc * LANES:(hc + 1) * LANES]
            for h in range(H):
                vo_ref[:, h, :] = q_refs[idx][:, h * W:(h + 1) * W]


def pack_kv(kfs, qkvs, H):
    n = len(kfs)
    M = kfs[0].shape[0]
    TR = 256
    nb = M // TR
    W = 2 * LANES
    HC = 2 * H
    v_blk = (qkvs[0].shape[1] - H * W) // (H * W)

    def frozen(idx, col):
        return lambda l, i: (jnp.where(l == idx, i, jnp.where(l > idx, nb - 1, 0)), col)

    return pl.pallas_call(
        functools.partial(_pack_kv_kernel, n=n, HC=HC, H=H, TR=TR),
        grid=(n, nb),
        in_specs=[pl.BlockSpec((TR, H * W), frozen(idx, 0)) for idx in range(n)]
        + [pl.BlockSpec((TR, H * W), frozen(idx, v_blk)) for idx in range(n)],
        out_specs=[pl.BlockSpec((None, TR * HC, LANES), lambda l, i: (l, i, 0)),
                   pl.BlockSpec((None, TR, H, W), lambda l, i: (l, i, 0, 0))],
        out_shape=[jax.ShapeDtypeStruct((n, M * HC, LANES), F32), jax.ShapeDtypeStruct((n, M, H, W), F32)],
        compiler_params=_cparams(("arbitrary", "arbitrary"), 32),
        name="pack_kv",
    )(*kfs, *qkvs)


def _rope_tables(pos):
    half = LANES // 2
    inv = 1.0 / (ROPE_THETA ** (jnp.arange(half, dtype=F32) * (2.0 / LANES)))
    ang = pos.astype(F32)[:, None] * inv[None, :]
    cos, sin = jnp.cos(ang), jnp.sin(ang)
    return jnp.concatenate([cos, cos], axis=-1), jnp.concatenate([-sin, sin], axis=-1)


def _diff_lambda_in_kernel(lam_ref, lam_init):
    lf = lam_ref[...]
    s1 = jnp.sum(lf[0:1, :] * lf[1:2, :], axis=-1, keepdims=True)
    s2 = jnp.sum(lf[2:3, :] * lf[3:4, :], axis=-1, keepdims=True)
    return jnp.exp(s1) - jnp.exp(s2) + lam_init


def _subln(o, g_ref, lam_init, eps=1e-5):
    y = o * lax.rsqrt(jnp.mean(o * o, axis=-1, keepdims=True) + eps)
    return (y * g_ref[...]) * (1.0 - lam_init)


def _flash_kernel(q_ref, k_ref, v_ref, lam_ref, g_ref, o_ref, m_scr, l_scr, acc_scr, *, TQ, TK, lam_init):
    qi = pl.program_id(2)
    DH = LANES
    m_scr[...] = jnp.full(m_scr.shape, -jnp.inf, F32)
    l_scr[...] = jnp.zeros(l_scr.shape, F32)
    acc_scr[...] = jnp.zeros(acc_scr.shape, F32)
    q = q_ref[...]
    n_lt = TK // LANES
    T = TQ

    def tile(kt, masked):
        r0 = pl.multiple_of(kt * TK, TK)
        k = k_ref[pl.ds(r0, TK), :]
        v = v_ref[pl.ds(r0, TK), :]
        ss = [lax.dot_general(q[:, c * DH:(c + 1) * DH], k[:, c * DH:(c + 1) * DH],
                              (((1,), (1,)), ((), ())), preferred_element_type=F32) for c in range(2)]
        if masked:
            rr = qi * TQ + lax.broadcasted_iota(jnp.int32, (TQ, TK), 0)
            cc = kt * TK + lax.broadcasted_iota(jnp.int32, (TQ, TK), 1)
            ss = [jnp.where(cc <= rr, s, -jnp.inf) for s in ss]
        ebs, corrs = [], []
        for c in range(2):
            s = ss[c]
            m_old = m_scr[c]
            m_new = jnp.maximum(m_old, jnp.max(s, axis=-1, keepdims=True))
            corr = jnp.exp2(m_old - m_new)
            e = [jnp.exp2(s[:, j * LANES:(j + 1) * LANES] - m_new) for j in range(n_lt)]
            rs = e[0]
            for j in range(1, n_lt):
                rs = rs + e[j]
            l_scr[c] = l_scr[c] * corr + jnp.sum(rs, axis=-1, keepdims=True)
            m_scr[c] = m_new
            ebs.append(jnp.concatenate([x.astype(BF16) for x in e], axis=1))
            corrs.append(corr)
        pv = jnp.dot(jnp.concatenate(ebs, axis=0), v, preferred_element_type=F32)
        for c in range(2):
            acc_scr[c] = acc_scr[c] * jnp.concatenate([corrs[c], corrs[c]], axis=1) + pv[c * T:(c + 1) * T]

    def body(kt, carry):
        tile(kt, False)
        return carry

    n_full = qi * (TQ // TK)
    lax.fori_loop(0, n_full, body, 0)
    for d in range(TQ // TK):
        tile(n_full + d, True)
    lam = _diff_lambda_in_kernel(lam_ref, lam_init)
    l0 = jnp.concatenate([l_scr[0], l_scr[0]], axis=1)
    l1 = jnp.concatenate([l_scr[1], l_scr[1]], axis=1)
    o = acc_scr[0] / l0 - lam * (acc_scr[1] / l1)
    o_ref[...] = _subln(o, g_ref, lam_init).astype(o_ref.dtype)


def flash_diff_attention(qb, kb, vb, lam_rows, subln, B, L, H, lam_init):
    T, TK = 512, 512
    W = 2 * LANES
    nq = L // T
    return pl.pallas_call(
        functools.partial(_flash_kernel, TQ=T, TK=TK, lam_init=lam_init),
        grid=(B, H, nq),
        in_specs=[pl.BlockSpec((T, W), lambda b, h, i: (b * nq + i, h)),
                  pl.BlockSpec((L, W), lambda b, h, i: (b, h)),
                  pl.BlockSpec((L, W), lambda b, h, i: (b, h)),
                  pl.BlockSpec((4, LANES), lambda b, h, i: (0, 0)),
                  pl.BlockSpec((1, W), lambda b, h, i: (0, 0))],
        out_specs=pl.BlockSpec((T, W), lambda b, h, i: (b * nq + i, h)),
        out_shape=jax.ShapeDtypeStruct((B * L, H * W), BF16),
        scratch_shapes=[pltpu.VMEM((2, T, LANES), F32), pltpu.VMEM((2, T, LANES), F32), pltpu.VMEM((2, T, W), F32)],
        compiler_params=_cparams(("parallel", "parallel", "arbitrary"), 40),
        name="flash_diff_attention",
    )(qb, kb, vb, lam_rows, subln.reshape(1, W))


def _paged_kernel(pt_ref, q_ref, kn_ref, vn_ref, lam_ref, g_ref, *rest, P, H, T, TP, n_steps, lam_init):
    k_refs = rest[:P]
    v_refs = rest[P:2 * P]
    o_ref = rest[2 * P]
    m_scr, l_scr, acc_scr = rest[2 * P + 1:]
    p = pl.program_id(1)
    NR = H * T
    q = q_ref[0]
    nt = functools.partial(lax.dot_general, dimension_numbers=(((1,), (1,)), ((), ())),
                           preferred_element_type=F32)

    def masked(s, causal):
        row = lax.broadcasted_iota(jnp.int32, s.shape, 0) % NR
        col = lax.broadcasted_iota(jnp.int32, s.shape, 1)
        ok = col % H == row // T
        if causal:
            ok = jnp.logical_and(ok, col // H <= row % T)
        return jnp.where(ok, s, -jnp.inf)

    def update(s, v):
        m_old = m_scr[...]
        m_new = jnp.maximum(m_old, jnp.max(s, axis=-1, keepdims=True))
        corr = jnp.exp(m_old - m_new)
        e = jnp.exp(s - m_new)
        l_scr[...] = l_scr[...] * corr + jnp.sum(e, axis=-1, keepdims=True)
        m_scr[...] = m_new
        acc_scr[...] = acc_scr[...] * corr + jnp.dot(e.astype(BF16), v, preferred_element_type=F32)

    @pl.when(p == 0)
    def _():
        m_scr[...] = jnp.full(m_scr.shape, -jnp.inf, F32)
        l_scr[...] = jnp.zeros(l_scr.shape, F32)
        acc_scr[...] = jnp.zeros(acc_scr.shape, F32)
        kn = kn_ref[0]
        s = jnp.concatenate([nt(q[c], kn[c]) for c in range(2)], axis=0)
        update(masked(s, True), vn_ref[0])

    ss = []
    for c in range(2):
        kc = jnp.concatenate([k_refs[r][pl.ds(c, TP * H, stride=2), :] for r in range(P)], axis=0)
        ss.append(nt(q[c], kc.astype(BF16)))
    v = jnp.concatenate([v_refs[r][...] for r in range(P)], axis=0).astype(BF16)
    update(masked(jnp.concatenate(ss, axis=0), False), v)

    @pl.when(p == n_steps - 1)
    def _():
        lam = _diff_lambda_in_kernel(lam_ref, lam_init)
        po = acc_scr[...] / l_scr[...]
        o = po[:NR, :] - lam * po[NR:, :]
        o_ref[0] = _subln(o, g_ref, lam_init).astype(o_ref.dtype)


def paged_diff_attention(qs, kn, vn, k_pool, v_pool, layer, page_table, lam_rows, subln, H, T, lam_init):
    B = qs.shape[0]
    n_pages = page_table.shape[1]
    TP = k_pool.shape[2] // (2 * H)
    P = 4
    n_steps = n_pages // P
    W = 2 * LANES
    NR = H * T

    def kmap(r):
        return lambda b, p, pt: (layer, pt[b, p * P + r], 0, 0)

    in_specs = [pl.BlockSpec((1, 2, NR, LANES), lambda b, p, pt: (b, 0, 0, 0)),
                pl.BlockSpec((1, 2, NR, LANES), lambda b, p, pt: (b, 0, 0, 0)),
                pl.BlockSpec((1, NR, W), lambda b, p, pt: (b, 0, 0)),
                pl.BlockSpec((4, LANES), lambda b, p, pt: (0, 0)),
                pl.BlockSpec((1, W), lambda b, p, pt: (0, 0))]
    in_specs += [pl.BlockSpec((None, None, TP * 2 * H, LANES), kmap(r)) for r in range(P)]
    in_specs += [pl.BlockSpec((None, None, TP * H, W), kmap(r)) for r in range(P)]
    return pl.pallas_call(
        functools.partial(_paged_kernel, P=P, H=H, T=T, TP=TP, n_steps=n_steps, lam_init=lam_init),
        grid_spec=pltpu.PrefetchScalarGridSpec(
            num_scalar_prefetch=1,
            grid=(B, n_steps),
            in_specs=in_specs,
            out_specs=pl.BlockSpec((1, NR, W), lambda b, p, pt: (b, 0, 0)),
            scratch_shapes=[pltpu.VMEM((2 * NR, 1), F32), pltpu.VMEM((2 * NR, 1), F32),
                            pltpu.VMEM((2 * NR, W), F32)],
        ),
        out_shape=jax.ShapeDtypeStruct((B, NR, W), BF16),
        compiler_params=_cparams(("parallel", "arbitrary"), 48),
        name="paged_diff_attention",
    )(page_table, qs, kn, vn, lam_rows, subln.reshape(1, W), *([k_pool] * P), *([v_pool] * P))


def _mem_attn_kernel(q_ref, k_ref, v_ref, o_ref, *, H, scale):
    q = q_ref[0]
    k = k_ref[0].astype(BF16)
    v = v_ref[0].astype(BF16)
    for h in range(H):
        sl = slice(h * LANES, (h + 1) * LANES)
        s = lax.dot_general(q[:, sl], k[:, sl], (((1,), (1,)), ((), ())), preferred_element_type=F32) * scale
        m = jnp.max(s, axis=-1, keepdims=True)
        e = jnp.exp(s - m)
        p = e / jnp.sum(e, axis=-1, keepdims=True)
        o_ref[0, :, sl] = jnp.dot(p.astype(BF16), v[:, sl], preferred_element_type=F32).astype(o_ref.dtype)


def mem_attention(q, mk, mv, H):
    B, L, D = q.shape
    Mm = mk.shape[1]
    tq = min(L, 512)
    return pl.pallas_call(
        functools.partial(_mem_attn_kernel, H=H, scale=LANES ** -0.5),
        grid=(B, L // tq),
        in_specs=[pl.BlockSpec((1, tq, D), lambda b, i: (b, i, 0)),
                  pl.BlockSpec((1, Mm, D), lambda b, i: (b, 0, 0)),
                  pl.BlockSpec((1, Mm, D), lambda b, i: (b, 0, 0))],
        out_specs=pl.BlockSpec((1, tq, D), lambda b, i: (b, i, 0)),
        out_shape=jax.ShapeDtypeStruct((B, L, D), BF16),
        compiler_params=_cparams(("parallel", "parallel"), 24),
        name="mem_attention",
    )(q, mk, mv)


def _gdn_core(h, ba, B, L, conv_state, s0, j, conv_w, a_log, dt_bias, norm_g, dims):
    HK, HV, n_qk, n_conv = dims
    q_scale = LANES ** -0.5
    if conv_state is None:
        qkv = conv_prompt(h, conv_w, B, L, n_qk, n_conv, q_scale)
        new_conv = h.reshape(B, L, -1)[:, L - (CONV_W - 1):, :n_conv]
        NC = L // CHUNK
        zsrc, z_col0 = h, n_conv
        beta, gc, gct = gdn_gates(ba, a_log, dt_bias, CHUNK, HV)
    else:
        u = h[:, :n_conv].reshape(B, L, n_conv)
        ut = jnp.swapaxes(u, 0, 1)
        st = jnp.swapaxes(conv_state, 0, 1)
        y = conv_sample(ut, st, conv_w, n_qk, q_scale)
        new_conv = jnp.concatenate([conv_state, u], axis=1)[:, L:, :]
        padrows = lambda t: jnp.pad(t.reshape(B, L, -1), ((0, 0), (0, CHUNK - L), (0, 0))).reshape(B * CHUNK, -1)
        qkv = padrows(jnp.swapaxes(y, 0, 1))
        zsrc, z_col0 = padrows(h[:, n_conv:]), 0
        NC = 1
        beta, gc, gct = gdn_gates(padrows(ba), a_log, dt_bias, L, HV)
    og, s_new = gdn_recurrence(qkv, zsrc, z_col0, beta, gc, gct, s0, j if s0.shape[0] > 1 else 0, norm_g,
                               B, NC, HK, HV)
    if conv_state is not None:
        og = og.reshape(B, CHUNK, -1)[:, :L].reshape(B * L, -1)
    return og, new_conv, s_new


def _forward(groups, params):
    (norm_mix, gdn_w_in, gdn_conv_w, gdn_a_log, gdn_dt_bias, gdn_norm, gdn_w_out, attn_w_qkv,
     attn_lambda, attn_subln, attn_w_o, norm_xattn, w_xq, w_xo, norm_ffn, w_up, w_down, norm_final) = params
    depth = norm_mix.shape[0]
    HV = gdn_a_log.shape[1]
    HK = HV // 2
    n_qk = HK * LANES
    n_conv = gdn_conv_w.shape[2]
    XH = w_xq.shape[2] // LANES
    xs = [g["x"] for g in groups]
    tables = []
    for g in groups:
        cos2, sin2 = _rope_tables(g["pos"])
        if g["L"] < 256:
            cos2, sin2 = jnp.tile(cos2, (g["B"], 1)), jnp.tile(sin2, (g["B"], 1))
        tables.append((cos2, sin2))
    outs = [dict(conv=[], delta=[], k=[], v=[]) for _ in groups]

    def both(acts, w, layer, res=None, **kw):
        return [matmul(a, w, layer, res=None if res is None else res[n], **kw) for n, a in enumerate(acts)]

    for i in range(depth):
        j = i // 2
        hn = [rmsnorm(x, norm_mix[i], BF16) for x in xs]
        if i % 2 == 0:
            h = both(hn, gdn_w_in, j, n_out=n_conv + HV * LANES, w_is_t=True)
            og = []
            for g, o, h_g, hn_g in zip(groups, outs, h, hn):
                ba = matmul_tail(hn_g, gdn_w_in, j, n_conv + HV * LANES)
                conv_state, s0 = g["gdn_state"](j)
                og_g, new_conv, s_new = _gdn_core(h_g, ba, g["B"], g["L"], conv_state, s0, j, gdn_conv_w[j],
                                                  gdn_a_log[j], gdn_dt_bias[j], gdn_norm[j], (HK, HV, n_qk, n_conv))
                og.append(og_g)
                o["conv"].append(new_conv)
                o["delta"].append(s_new)
            xs = both(og, gdn_w_out, j, res=xs)
        else:
            lam_init = 0.8 - 0.6 * math.exp(-0.3 * i)
            n_qk_a = attn_w_qkv.shape[2] // 3
            qkv = both(hn, attn_w_qkv, j)
            oa = []
            for g, o, qkv_g, (cos2, sin2) in zip(groups, outs, qkv, tables):
                qb, kf, kb, vb = rope_split(qkv_g, cos2, sin2, n_qk_a, n_qk_a, g["q_scale"])
                oa.append(g["attn"](j, qb, kb, vb, attn_lambda[j], attn_subln[j], lam_init))
                o["k"].append(kf)
                o["v"].append(qkv_g)
            xs = both(oa, attn_w_o, j, res=xs)
        hx = [rmsnorm(x, norm_xattn[i], BF16) for x in xs]
        q = both(hx, w_xq, i, out_dtype=BF16)
        om = []
        for g, q_g in zip(groups, q):
            B, L = g["B"], g["L"]
            Lq = max(L, SLOT_ROWS)
            mk, mv = g["mem_kv"](i)
            q3 = jnp.pad(q_g.reshape(B, L, -1), ((0, 0), (0, Lq - L), (0, 0)))
            om.append(mem_attention(q3, mk, mv, XH)[:, :L].reshape(B * L, -1))
        xs = both(om, w_xo, i, res=xs)
        hf = [rmsnorm(x, norm_ffn[i], BF16) for x in xs]
        up = both(hf, w_up, i, out_dtype=BF16, act="relu2")
        xs = both(up, w_down, i, res=xs)
    ys = [rmsnorm(x, norm_final, F32) for x in xs]
    return ys, outs


def kernel(x_prompt, x_sample, mem_prompt, state_conv, state_delta, cache_k, cache_v, cache_mem_k, cache_mem_v, page_table, norm_mix, gdn_w_in, gdn_conv_w, gdn_a_log, gdn_dt_bias, gdn_norm, gdn_w_out, attn_w_qkv, attn_lambda, attn_subln, attn_w_o, norm_xattn, norm_mem, w_xq, w_xkv, w_xo, norm_ffn, w_up, w_down, norm_final):
    Bp, Lp, D = x_prompt.shape
    Bs, Ls, _ = x_sample.shape
    depth = norm_mix.shape[0]
    HV = gdn_a_log.shape[1]
    n_conv = gdn_conv_w.shape[2]
    n_attn, n_pool, TP, H = cache_k.shape[:4]
    W = 2 * LANES
    XH = w_xq.shape[2] // LANES
    Mm = mem_prompt.shape[1]
    past_len = page_table.shape[1] * TP

    gdn_w_in_t = jnp.swapaxes(gdn_w_in, 1, 2)
    params = (norm_mix, gdn_w_in_t, gdn_conv_w, gdn_a_log, gdn_dt_bias, gdn_norm, gdn_w_out, attn_w_qkv,
              attn_lambda, attn_subln, attn_w_o, norm_xattn, w_xq, w_xo, norm_ffn, w_up, w_down, norm_final)

    p_mk, p_mv = [], []

    def prompt_mem(i):
        mn = rmsnorm(mem_prompt.reshape(Bp * Mm, D), norm_mem[i], BF16)
        kv = matmul(mn, w_xkv, i).reshape(Bp, Mm, 2 * XH * LANES)
        mk, mv = kv[:, :, :XH * LANES], kv[:, :, XH * LANES:]
        p_mk.append(mk.reshape(Bp, Mm, XH, LANES))
        p_mv.append(mv.reshape(Bp, Mm, XH, LANES))
        return mk, mv

    def prompt_attn(j, qb, kb, vb, lam_rows, subln, lam_init):
        return flash_diff_attention(qb, kb, vb, lam_rows, subln, Bp, Lp, H, lam_init)

    zero_state = jnp.zeros((1, Bp, HV, LANES, LANES), F32)
    prompt = dict(x=x_prompt.reshape(Bp * Lp, D), B=Bp, L=Lp, pos=jnp.arange(Lp), mem_kv=prompt_mem,
                  gdn_state=lambda j: (None, zero_state), attn=prompt_attn,
                  q_scale=LANES ** -0.5 * math.log2(math.e))

    k_pool = cache_k.reshape(n_attn, n_pool, TP * H * 2, LANES)
    v_pool = cache_v.reshape(n_attn, n_pool, TP * H, W)

    def sample_attn(j, qb, kb, vb, lam_rows, subln, lam_init):
        qs = jnp.transpose(qb.reshape(Bs, Ls, H, 2, LANES), (0, 3, 2, 1, 4)).reshape(Bs, 2, H * Ls, LANES)
        kn = jnp.transpose(kb.reshape(Bs, Ls, H, 2, LANES), (0, 3, 1, 2, 4)).reshape(Bs, 2, Ls * H, LANES)
        oa = paged_diff_attention(qs, kn, vb.reshape(Bs, Ls * H, W), k_pool, v_pool, j, page_table,
                                  lam_rows, subln, H, Ls, lam_init)
        return jnp.swapaxes(oa.reshape(Bs, H, Ls, W), 1, 2).reshape(Bs * Ls, H * W)

    sample = dict(x=x_sample.reshape(Bs * Ls, D), B=Bs, L=Ls, pos=past_len + jnp.arange(Ls),
                  mem_kv=lambda i: (cache_mem_k[i].reshape(Bs, Mm, -1), cache_mem_v[i].reshape(Bs, Mm, -1)),
                  gdn_state=lambda j: (state_conv[j], state_delta), attn=sample_attn, q_scale=LANES ** -0.5)

    (yp, ys), (po, so) = _forward([prompt, sample], params)

    pk_all, pv_all = pack_kv(po["k"], po["v"], H)
    s_v = [t[:, t.shape[1] - H * W:] for t in so["v"]]
    return (yp.reshape(Bp, Lp, D), ys.reshape(Bs, Ls, D),
            jnp.stack(po["conv"]), jnp.stack(po["delta"]), jnp.stack(so["conv"]), jnp.stack(so["delta"]),
            pk_all.reshape(n_attn, Bp, Lp, H, 2, LANES), pv_all.reshape(n_attn, Bp, Lp, H, W),
            jnp.stack(so["k"]).reshape(n_attn, Bs, Ls, H, 2, LANES), jnp.stack(s_v).reshape(n_attn, Bs, Ls, H, W),
            jnp.stack(p_mk), jnp.stack(p_mv))
```

```python
import functools
import math

import jax
import jax.numpy as jnp
from jax import lax
from jax.experimental import pallas as pl
from jax.experimental.pallas import tpu as pltpu

F32 = jnp.float32
BF16 = jnp.bfloat16

LANES = 128
CHUNK = 64
CONV_W = 4
EPS = 1e-6
ROPE_THETA = 10000.0
VMEM_CAP = 60 * 1024 * 1024
SLOT_ROWS = 16


def _cparams(sem, vmem_mb):
    return pltpu.CompilerParams(dimension_semantics=sem, vmem_limit_bytes=min(VMEM_CAP, vmem_mb * 1024 * 1024))


def _rmsnorm_kernel(x_ref, g_ref, o_ref, *, eps):
    x = x_ref[...]
    y = x * lax.rsqrt(jnp.mean(x * x, axis=-1, keepdims=True) + eps)
    o_ref[...] = (y * g_ref[...]).astype(o_ref.dtype)


def rmsnorm(x, gain, out_dtype, eps=EPS):
    M, D = x.shape
    tm = min(M, 512)
    return pl.pallas_call(
        functools.partial(_rmsnorm_kernel, eps=eps),
        grid=(M // tm,),
        in_specs=[pl.BlockSpec((tm, D), lambda i: (i, 0)), pl.BlockSpec((1, D), lambda i: (0, 0))],
        out_specs=pl.BlockSpec((tm, D), lambda i: (i, 0)),
        out_shape=jax.ShapeDtypeStruct((M, D), out_dtype),
        compiler_params=_cparams(("parallel",), 32),
        name="rmsnorm",
    )(x, gain.reshape(1, D))


def _mm_kernel(*refs, nk, act, has_res, w_is_t):
    it = iter(refs)
    a_ref = next(it)
    w_ref = next(it)
    r_ref = next(it) if has_res else None
    o_ref = next(it)
    acc_ref = next(it) if nk > 1 else None
    contract_w = 1 if w_is_t else 0
    p = lax.dot_general(a_ref[...].astype(BF16), w_ref[...].astype(BF16), (((1,), (contract_w,)), ((), ())),
                        preferred_element_type=F32)

    def finish(acc):
        if act == "relu2":
            r = jnp.maximum(acc, 0.0)
            acc = r * r
        if has_res:
            acc = acc + r_ref[...]
        o_ref[...] = acc.astype(o_ref.dtype)

    if nk == 1:
        finish(p)
    else:
        k = pl.program_id(2)

        @pl.when(k == 0)
        def _():
            acc_ref[...] = p

        @pl.when(k > 0)
        def _():
            acc_ref[...] += p

        @pl.when(k == nk - 1)
        def _():
            finish(acc_ref[...])


def matmul(a, w, layer, *, n_out=None, out_dtype=F32, act=None, res=None, w_is_t=False):
    M, K = a.shape
    N = n_out if n_out is not None else w.shape[1 if w_is_t else 2]
    if M >= 2048 and K <= 2048:
        tm, tn, tk = 2048, min(N, 512), K
    elif M >= 2048 and K <= 4096:
        tm, tn, tk = 1024, min(N, 512), K
    elif M >= 2048:
        tm, tn, tk = 2048, min(N, 512), 2048
    else:
        tm, tn, tk = M, min(N, 1024), min(K, 2048)
    nk = K // tk
    if w_is_t:
        w_spec = pl.BlockSpec((None, tn, tk), lambda i, j, k: (layer, j, k))
    else:
        w_spec = pl.BlockSpec((None, tk, tn), lambda i, j, k: (layer, k, j))
    in_specs = [pl.BlockSpec((tm, tk), lambda i, j, k: (i, k)), w_spec]
    args = [a, w]
    if res is not None:
        in_specs.append(pl.BlockSpec((tm, tn), lambda i, j, k: (i, j)))
        args.append(res)
    out_b = jnp.dtype(out_dtype).itemsize
    est = 2 * (tm * tk * a.dtype.itemsize + tk * tn * 4 + tm * tn * out_b + (tm * tn * 4 if res is not None else 0))
    est += tm * tn * 4 * (2 if nk > 1 else 1) + tk * tn * 2
    return pl.pallas_call(
        functools.partial(_mm_kernel, nk=nk, act=act, has_res=res is not None, w_is_t=w_is_t),
        grid=(M // tm, N // tn, nk),
        in_specs=in_specs,
        out_specs=pl.BlockSpec((tm, tn), lambda i, j, k: (i, j)),
        out_shape=jax.ShapeDtypeStruct((M, N), out_dtype),
        scratch_shapes=[pltpu.VMEM((tm, tn), F32)] if nk > 1 else [],
        compiler_params=_cparams(("parallel", "parallel", "arbitrary"), est // (1024 * 1024) + 8),
        name="matmul",
    )(*args)


def _mm_tail_kernel(a_ref, w_ref, o_ref):
    p = lax.dot_general(a_ref[...], w_ref[...].astype(BF16), (((1,), (1,)), ((), ())), preferred_element_type=F32)
    o_ref[...] = jnp.concatenate([p, jnp.zeros((p.shape[0], LANES - p.shape[1]), F32)], axis=1)


def matmul_tail(a, wt, layer, row0):
    M, K = a.shape
    n_tail = wt.shape[1] - row0
    tm = min(M, 2048)
    return pl.pallas_call(
        _mm_tail_kernel,
        grid=(M // tm,),
        in_specs=[pl.BlockSpec((tm, K), lambda i: (i, 0)),
                  pl.BlockSpec((None, n_tail, K), lambda i: (layer, row0 // n_tail, 0))],
        out_specs=pl.BlockSpec((tm, LANES), lambda i: (i, 0)),
        out_shape=jax.ShapeDtypeStruct((M, LANES), F32),
        compiler_params=_cparams(("parallel",), 32),
        name="matmul_tail",
    )(a, wt)


def _l2_groups(y, scale):
    outs = []
    for g in range(y.shape[-1] // LANES):
        yg = y[:, g * LANES:(g + 1) * LANES]
        ss = jnp.sum(yg * yg, axis=-1, keepdims=True)
        outs.append(yg * (lax.rsqrt(ss + 1e-6) * scale))
    return outs


def _conv_post(y, o_ref, rows, j, n_q, n_k, q_scale):
    y = y * jax.nn.sigmoid(y)

    @pl.when(j < n_q)
    def _():
        for g, yg in enumerate(_l2_groups(y, q_scale)):
            o_ref[rows, g * LANES:(g + 1) * LANES] = yg

    @pl.when(jnp.logical_and(j >= n_q, j < n_q + n_k))
    def _():
        for g, yg in enumerate(_l2_groups(y, 1.0)):
            o_ref[rows, g * LANES:(g + 1) * LANES] = yg

    @pl.when(j >= n_q + n_k)
    def _():
        o_ref[rows, :] = y


def _conv_prompt_kernel(x_ref, w_ref, o_ref, *, L, TR, n_q, n_k, q_scale):
    j = pl.program_id(1)
    w = w_ref[...]

    def run(l2_scale):
        def body(i, carry):
            r0 = pl.multiple_of(i * TR, TR)
            xa = x_ref[pl.ds(r0, TR), :]
            pstart = pl.multiple_of(jnp.maximum(r0 - 8, 0), 8)
            xp = jnp.where(i == 0, 0.0, x_ref[pl.ds(pstart, 8), :])
            xx = jnp.concatenate([xp, xa], axis=0)
            y = xa * w[3:4, :]
            for s in range(1, CONV_W):
                y = y + pltpu.roll(xx, s, axis=0)[8:, :] * w[3 - s:4 - s, :]
            y = y * jax.nn.sigmoid(y)
            if l2_scale is None:
                o_ref[pl.ds(r0, TR), :] = y
            else:
                for g, yg in enumerate(_l2_groups(y, l2_scale)):
                    o_ref[pl.ds(r0, TR), g * LANES:(g + 1) * LANES] = yg
            return carry

        lax.fori_loop(0, L // TR, body, 0)

    pl.when(j < n_q)(lambda: run(q_scale))
    pl.when(jnp.logical_and(j >= n_q, j < n_q + n_k))(lambda: run(1.0))
    pl.when(j >= n_q + n_k)(lambda: run(None))


def conv_prompt(h, conv_w, B, L, n_qk_ch, n_ch, q_scale):
    tc = 256
    n_q = n_qk_ch // tc
    return pl.pallas_call(
        functools.partial(_conv_prompt_kernel, L=L, TR=256, n_q=n_q, n_k=n_q, q_scale=q_scale),
        grid=(B, n_ch // tc),
        in_specs=[pl.BlockSpec((L, tc), lambda b, j: (b, j)), pl.BlockSpec((CONV_W, tc), lambda b, j: (0, j))],
        out_specs=pl.BlockSpec((L, tc), lambda b, j: (b, j)),
        out_shape=jax.ShapeDtypeStruct((B * L, n_ch), F32),
        compiler_params=_cparams(("parallel", "parallel"), 40),
        name="conv_prompt",
    )(h, conv_w)


def _conv_sample_kernel(u_ref, st_ref, w_ref, o_ref, *, T, n_q, n_k, q_scale):
    j = pl.program_id(0)
    w = w_ref[...]
    xc = [st_ref[i] for i in range(CONV_W - 1)] + [u_ref[t] for t in range(T)]
    for t in range(T):
        y = xc[t] * w[0:1, :]
        for i in range(1, CONV_W):
            y = y + xc[t + i] * w[i:i + 1, :]
        _conv_post(y, o_ref.at[t], slice(None), j, n_q, n_k, q_scale)


def conv_sample(u, state, conv_w, n_qk_ch, q_scale):
    T, B, C = u.shape
    tc = 1024
    n_q = n_qk_ch // tc
    return pl.pallas_call(
        functools.partial(_conv_sample_kernel, T=T, n_q=n_q, n_k=n_q, q_scale=q_scale),
        grid=(C // tc,),
        in_specs=[pl.BlockSpec((T, B, tc), lambda j: (0, 0, j)),
                  pl.BlockSpec((CONV_W - 1, B, tc), lambda j: (0, 0, j)),
                  pl.BlockSpec((CONV_W, tc), lambda j: (0, j))],
        out_specs=pl.BlockSpec((T, B, tc), lambda j: (0, 0, j)),
        out_shape=jax.ShapeDtypeStruct((T, B, C), F32),
        compiler_params=_cparams(("parallel",), 16),
        name="conv_sample",
    )(u, state, conv_w)


def _gates_kernel(ba_ref, alog_ref, dtb_ref, beta_ref, gc_ref, gct_ref, *, valid, n_heads):
    b = ba_ref[...]
    a = pltpu.roll(b, LANES - n_heads, axis=1)
    tb = b.shape[0]
    rc = lax.broadcasted_iota(jnp.int32, (tb, LANES), 0) % CHUNK
    ok = rc < valid
    beta = jnp.where(ok, jax.nn.sigmoid(b), 0.0)
    z = a + dtb_ref[...]
    softplus = jnp.maximum(z, 0.0) + jnp.log1p(jnp.exp(-jnp.abs(z)))
    g = jnp.where(ok, -jnp.exp(alog_ref[...]) * softplus, 0.0)
    s = 1
    while s < CHUNK:
        g = g + jnp.where(rc >= s, pltpu.roll(g, s, axis=0), 0.0)
        s *= 2
    beta_ref[...] = beta
    gc_ref[...] = g
    g3 = g.reshape(tb // CHUNK, CHUNK, LANES)
    gz = jnp.concatenate([g3, jnp.zeros_like(g3)], axis=1).reshape(2 * tb, LANES)
    gct_ref[...] = gz.T[:n_heads, :]


def gdn_gates(ba, a_log, dt_bias, valid, n_heads):
    M = ba.shape[0]
    tb = min(M, 512)
    pad = lambda v: jnp.pad(v.astype(F32), (0, LANES - n_heads)).reshape(1, LANES)
    return pl.pallas_call(
        functools.partial(_gates_kernel, valid=valid, n_heads=n_heads),
        grid=(M // tb,),
        in_specs=[pl.BlockSpec((tb, LANES), lambda i: (i, 0)),
                  pl.BlockSpec((1, LANES), lambda i: (0, 0)),
                  pl.BlockSpec((1, LANES), lambda i: (0, 0))],
        out_specs=[pl.BlockSpec((tb, LANES), lambda i: (i, 0)),
                   pl.BlockSpec((tb, LANES), lambda i: (i, 0)),
                   pl.BlockSpec((n_heads, 2 * tb), lambda i: (0, i))],
        out_shape=[jax.ShapeDtypeStruct((M, LANES), F32),
                   jax.ShapeDtypeStruct((M, LANES), F32),
                   jax.ShapeDtypeStruct((n_heads, 2 * M), F32)],
        compiler_params=_cparams(("parallel",), 16),
        name="gdn_gates",
    )(ba, pad(a_log), pad(dt_bias))


def _bdot(a, b):
    return jnp.dot(a.astype(BF16), b.astype(BF16), preferred_element_type=F32)


def _bdot_nt(a, b):
    return lax.dot_general(a.astype(BF16), b.astype(BF16), (((1,), (1,)), ((), ())), preferred_element_type=F32)


def _bdot_tn(a, b):
    return lax.dot_general(a.astype(BF16), b.astype(BF16), (((0,), (0,)), ((), ())), preferred_element_type=F32)


def _gdn_kernel(q_ref, k_ref, v_ref, z_ref, beta_ref, gc_ref, gr_ref, s0_ref, ng_ref, o_ref, so_ref, s_scr,
                *, G, NC, eps):
    hg = pl.program_id(1)
    n = pl.program_id(2)
    C = CHUNK

    @pl.when(n == 0)
    def _():
        s_scr[...] = s0_ref[0]

    row = lax.broadcasted_iota(jnp.int32, (C, C), 0)
    col = lax.broadcasted_iota(jnp.int32, (C, C), 1)
    incl = row >= col
    strict = row > col
    eye = (row == col).astype(F32)
    lane = lax.broadcasted_iota(jnp.int32, (C, LANES), 1)
    beta_all = beta_ref[...]
    gc_all = gc_ref[...]
    ng = ng_ref[...]
    heads = range(G)
    dot = functools.partial(jnp.dot, preferred_element_type=F32)

    qs = [q_ref[:, kh * LANES:(kh + 1) * LANES] for kh in range(G // 2)]
    ks = [k_ref[:, kh * LANES:(kh + 1) * LANES] for kh in range(G // 2)]
    qkk = [_bdot_nt(jnp.concatenate([qs[kh], ks[kh]], axis=0), ks[kh]) for kh in range(G // 2)]

    beta, eg, et, egl, dec, a, t = [], [], [], [], [], [], []
    for g in heads:
        sel = lane == hg * G + g
        b_g = jnp.sum(jnp.where(sel, beta_all, 0.0), axis=-1, keepdims=True)
        gc = jnp.sum(jnp.where(sel, gc_all, 0.0), axis=-1, keepdims=True)
        gl = gc[C - 1:C, :]
        gr = gr_ref[g:g + 1, :][:, :C]
        d_g = jnp.exp(jnp.where(incl, gc - gr, -jnp.inf))
        a_g = b_g * qkk[g // 2][C:] * jnp.where(strict, d_g, 0.0)
        beta.append(b_g)
        eg.append(jnp.exp(gc))
        et.append(jnp.exp(gl - gc))
        egl.append(jnp.exp(gl))
        dec.append(d_g)
        a.append(a_g)
        t.append(eye - jnp.where(row // 2 == col // 2, a_g, 0.0))

    s = 2
    while s < C:
        fmask = jnp.logical_and(row // (2 * s) == col // (2 * s), row // s != col // s)
        tb = [t[g].astype(BF16) for g in heads]
        x = [dot(tb[g], jnp.where(fmask, a[g], 0.0).astype(BF16)) for g in heads]
        t = [t[g] - dot(x[g].astype(BF16), tb[g]) for g in heads]
        s *= 2

    sol = []
    for g in heads:
        v = v_ref[:, g * LANES:(g + 1) * LANES]
        rhs = jnp.concatenate([beta[g] * v, (beta[g] * eg[g]) * ks[g // 2]], axis=1)
        sol.append(_bdot(t[g], rhs))
    S = [s_scr[g] for g in heads]
    wq = [_bdot(jnp.concatenate([sol[g][:, LANES:], qs[g // 2]], axis=0), S[g]) for g in heads]
    u = [sol[g][:, :LANES] - wq[g][:C] for g in heads]
    o1 = [_bdot(qkk[g // 2][:C] * dec[g], u[g]) for g in heads]
    for g in heads:
        s_scr[g] = egl[g] * S[g] + _bdot_tn(ks[g // 2] * et[g], u[g])
    for g in heads:
        o = eg[g] * wq[g][C:] + o1[g]
        on = o * lax.rsqrt(jnp.mean(o * o, axis=-1, keepdims=True) + eps) * ng
        zz = z_ref[:, g * LANES:(g + 1) * LANES]
        o_ref[:, g * LANES:(g + 1) * LANES] = (on * (zz * jax.nn.sigmoid(zz))).astype(o_ref.dtype)

    @pl.when(n == NC - 1)
    def _():
        so_ref[0] = s_scr[...]


def gdn_recurrence(qkv, zsrc, z_col0, beta, gc, gct, s0, layer, norm_g, B, NC, HK, HV, G=32):
    rows = B * NC * CHUNK
    GK = G // 2
    n_g = HV // G
    kq_w = GK * LANES
    v_w = G * LANES
    k_blk0 = (HK * LANES) // kq_w
    v_blk0 = (2 * HK * LANES) // v_w
    z_blk0 = z_col0 // v_w
    rmap = lambda b, g, n: b * NC + n
    return pl.pallas_call(
        functools.partial(_gdn_kernel, G=G, NC=NC, eps=EPS),
        grid=(B, n_g, NC),
        in_specs=[
            pl.BlockSpec((CHUNK, kq_w), lambda b, g, n: (rmap(b, g, n), g)),
            pl.BlockSpec((CHUNK, kq_w), lambda b, g, n: (rmap(b, g, n), k_blk0 + g)),
            pl.BlockSpec((CHUNK, v_w), lambda b, g, n: (rmap(b, g, n), v_blk0 + g)),
            pl.BlockSpec((CHUNK, v_w), lambda b, g, n: (rmap(b, g, n), z_blk0 + g)),
            pl.BlockSpec((CHUNK, LANES), lambda b, g, n: (rmap(b, g, n), 0)),
            pl.BlockSpec((CHUNK, LANES), lambda b, g, n: (rmap(b, g, n), 0)),
            pl.BlockSpec((G, LANES), lambda b, g, n: (g, rmap(b, g, n))),
            pl.BlockSpec((None, 1, G, LANES, LANES), lambda b, g, n: (layer, b, g, 0, 0)),
            pl.BlockSpec((1, LANES), lambda b, g, n: (0, 0)),
        ],
        out_specs=[
            pl.BlockSpec((CHUNK, v_w), lambda b, g, n: (rmap(b, g, n), g)),
            pl.BlockSpec((1, G, LANES, LANES), lambda b, g, n: (b, g, 0, 0)),
        ],
        out_shape=[jax.ShapeDtypeStruct((rows, HV * LANES), BF16),
                   jax.ShapeDtypeStruct((B, HV, LANES, LANES), F32)],
        scratch_shapes=[pltpu.VMEM((G, LANES, LANES), F32)],
        compiler_params=_cparams(("parallel", "parallel", "arbitrary"), 32),
        name="gdn_recurrence",
    )(qkv, qkv, qkv, zsrc, beta, gc, gct, s0, norm_g.reshape(1, LANES))


def _rope_kernel(x_ref, cos_ref, sin_ref, qb_ref, kf_ref, kb_ref, vb_ref, *, n_q, n_k, q_scale):
    cos = cos_ref[...]
    sin = sin_ref[...]
    for g in range(n_q + n_k):
        x = x_ref[:, g * LANES:(g + 1) * LANES]
        y = x * cos + pltpu.roll(x, LANES // 2, axis=1) * sin
        if g < n_q:
            if q_scale != 1.0:
                y = y * q_scale
            qb_ref[:, g * LANES:(g + 1) * LANES] = y.astype(BF16)
        else:
            kf_ref[:, (g - n_q) * LANES:(g - n_q + 1) * LANES] = y
            kb_ref[:, (g - n_q) * LANES:(g - n_q + 1) * LANES] = y.astype(BF16)
    vb_ref[...] = x_ref[:, (n_q + n_k) * LANES:].astype(BF16)


def rope_split(qkv, cos2, sin2, n_qk_ch, n_v_ch, q_scale):
    M = qkv.shape[0]
    Lc = cos2.shape[0]
    tr = min(M, 256)
    nb = Lc // tr
    n_q = n_qk_ch // LANES
    return pl.pallas_call(
        functools.partial(_rope_kernel, n_q=n_q, n_k=n_q, q_scale=q_scale),
        grid=(M // tr,),
        in_specs=[pl.BlockSpec((tr, qkv.shape[1]), lambda i: (i, 0)),
                  pl.BlockSpec((tr, LANES), lambda i: (i % nb, 0)),
                  pl.BlockSpec((tr, LANES), lambda i: (i % nb, 0))],
        out_specs=[pl.BlockSpec((tr, n_qk_ch), lambda i: (i, 0)),
                   pl.BlockSpec((tr, n_qk_ch), lambda i: (i, 0)),
                   pl.BlockSpec((tr, n_qk_ch), lambda i: (i, 0)),
                   pl.BlockSpec((tr, n_v_ch), lambda i: (i, 0))],
        out_shape=[jax.ShapeDtypeStruct((M, n_qk_ch), BF16),
                   jax.ShapeDtypeStruct((M, n_qk_ch), F32),
                   jax.ShapeDtypeStruct((M, n_qk_ch), BF16),
                   jax.ShapeDtypeStruct((M, n_v_ch), BF16)],
        compiler_params=_cparams(("parallel",), 40),
        name="rope_split",
    )(qkv, cos2, sin2)


def _pack_kv_kernel(*refs, n, HC, H, TR):
    k_refs, q_refs = refs[:n], refs[n:2 * n]
    ko_ref, vo_ref = refs[2 * n:]
    layer = pl.program_id(0)
    W = 2 * LANES
    for idx in range(n):
        @pl.when(layer == idx)
        def _():
            for hc in range(HC):
                ko_ref[pl.ds(hc, TR, stride=HC), :] = k_refs[idx][:, hc * LANES:(hc + 1) * LANES]
            for h in range(H):
                vo_ref[:, h, :] = q_refs[idx][:, h * W:(h + 1) * W]


def pack_kv(kfs, qkvs, H):
    n = len(kfs)
    M = kfs[0].shape[0]
    TR = 256
    nb = M // TR
    W = 2 * LANES
    HC = 2 * H
    v_blk = (qkvs[0].shape[1] - H * W) // (H * W)

    def frozen(idx, col):
        return lambda l, i: (jnp.where(l == idx, i, jnp.where(l > idx, nb - 1, 0)), col)

    return pl.pallas_call(
        functools.partial(_pack_kv_kernel, n=n, HC=HC, H=H, TR=TR),
        grid=(n, nb),
        in_specs=[pl.BlockSpec((TR, H * W), frozen(idx, 0)) for idx in range(n)]
        + [pl.BlockSpec((TR, H * W), frozen(idx, v_blk)) for idx in range(n)],
        out_specs=[pl.BlockSpec((None, TR * HC, LANES), lambda l, i: (l, i, 0)),
                   pl.BlockSpec((None, TR, H, W), lambda l, i: (l, i, 0, 0))],
        out_shape=[jax.ShapeDtypeStruct((n, M * HC, LANES), F32), jax.ShapeDtypeStruct((n, M, H, W), F32)],
        compiler_params=_cparams(("arbitrary", "arbitrary"), 32),
        name="pack_kv",
    )(*kfs, *qkvs)


def _rope_tables(pos):
    half = LANES // 2
    inv = 1.0 / (ROPE_THETA ** (jnp.arange(half, dtype=F32) * (2.0 / LANES)))
    ang = pos.astype(F32)[:, None] * inv[None, :]
    cos, sin = jnp.cos(ang), jnp.sin(ang)
    return jnp.concatenate([cos, cos], axis=-1), jnp.concatenate([-sin, sin], axis=-1)


def _diff_lambda_in_kernel(lam_ref, lam_init):
    lf = lam_ref[...]
    s1 = jnp.sum(lf[0:1, :] * lf[1:2, :], axis=-1, keepdims=True)
    s2 = jnp.sum(lf[2:3, :] * lf[3:4, :], axis=-1, keepdims=True)
    return jnp.exp(s1) - jnp.exp(s2) + lam_init


def _subln(o, g_ref, lam_init, eps=1e-5):
    y = o * lax.rsqrt(jnp.mean(o * o, axis=-1, keepdims=True) + eps)
    return (y * g_ref[...]) * (1.0 - lam_init)


def _flash_kernel(q_ref, k_ref, v_ref, lam_ref, g_ref, o_ref, m_scr, l_scr, acc_scr, *, TQ, TK, lam_init):
    qi = pl.program_id(2)
    DH = LANES
    m_scr[...] = jnp.full(m_scr.shape, -jnp.inf, F32)
    l_scr[...] = jnp.zeros(l_scr.shape, F32)
    acc_scr[...] = jnp.zeros(acc_scr.shape, F32)
    q = q_ref[...]
    n_lt = TK // LANES
    T = TQ

    def tile(kt, masked):
        r0 = pl.multiple_of(kt * TK, TK)
        k = k_ref[pl.ds(r0, TK), :]
        v = v_ref[pl.ds(r0, TK), :]
        ss = [lax.dot_general(q[:, c * DH:(c + 1) * DH], k[:, c * DH:(c + 1) * DH],
                              (((1,), (1,)), ((), ())), preferred_element_type=F32) for c in range(2)]
        if masked:
            rr = lax.broadcasted_iota(jnp.int32, (TQ, TK), 0)
            cc = lax.broadcasted_iota(jnp.int32, (TQ, TK), 1)
            if TQ != TK:
                rr, cc = qi * TQ + rr, kt * TK + cc
            ss = [jnp.where(cc <= rr, s, -jnp.inf) for s in ss]
        ebs, corrs = [], []
        for c in range(2):
            s = ss[c]
            m_old = m_scr[c]
            m_new = jnp.maximum(m_old, jnp.max(s, axis=-1, keepdims=True))
            corr = jnp.exp2(m_old - m_new)
            e = [jnp.exp2(s[:, j * LANES:(j + 1) * LANES] - m_new) for j in range(n_lt)]
            rs = e[0]
            for j in range(1, n_lt):
                rs = rs + e[j]
            l_scr[c] = l_scr[c] * corr + jnp.sum(rs, axis=-1, keepdims=True)
            m_scr[c] = m_new
            ebs.append(jnp.concatenate([x.astype(BF16) for x in e], axis=1))
            corrs.append(corr)
        pv = jnp.dot(jnp.concatenate(ebs, axis=0), v, preferred_element_type=F32)
        for c in range(2):
            acc_scr[c] = acc_scr[c] * jnp.concatenate([corrs[c], corrs[c]], axis=1) + pv[c * T:(c + 1) * T]

    def body(kt, carry):
        tile(kt, False)
        return carry

    n_full = qi * (TQ // TK)
    lax.fori_loop(0, n_full, body, 0)
    for d in range(TQ // TK):
        tile(n_full + d, True)
    lam = _diff_lambda_in_kernel(lam_ref, lam_init)
    l0 = jnp.concatenate([l_scr[0], l_scr[0]], axis=1)
    l1 = jnp.concatenate([l_scr[1], l_scr[1]], axis=1)
    o = acc_scr[0] / l0 - lam * (acc_scr[1] / l1)
    o_ref[...] = _subln(o, g_ref, lam_init).astype(o_ref.dtype)


def flash_diff_attention(qb, kb, vb, lam_rows, subln, B, L, H, lam_init):
    T, TK = 512, 512
    W = 2 * LANES
    nq = L // T
    return pl.pallas_call(
        functools.partial(_flash_kernel, TQ=T, TK=TK, lam_init=lam_init),
        grid=(B, H, nq),
        in_specs=[pl.BlockSpec((T, W), lambda b, h, i: (b * nq + i, h)),
                  pl.BlockSpec((L, W), lambda b, h, i: (b, h)),
                  pl.BlockSpec((L, W), lambda b, h, i: (b, h)),
                  pl.BlockSpec((4, LANES), lambda b, h, i: (0, 0)),
                  pl.BlockSpec((1, W), lambda b, h, i: (0, 0))],
        out_specs=pl.BlockSpec((T, W), lambda b, h, i: (b * nq + i, h)),
        out_shape=jax.ShapeDtypeStruct((B * L, H * W), BF16),
        scratch_shapes=[pltpu.VMEM((2, T, LANES), F32), pltpu.VMEM((2, T, LANES), F32), pltpu.VMEM((2, T, W), F32)],
        compiler_params=_cparams(("parallel", "parallel", "arbitrary"), 40),
        name="flash_diff_attention",
    )(qb, kb, vb, lam_rows, subln.reshape(1, W))


def _paged_kernel(pt_ref, q_ref, kn_ref, vn_ref, lam_ref, g_ref, *rest, P, H, T, TP, n_steps, lam_init):
    k_refs = rest[:P]
    v_refs = rest[P:2 * P]
    o_ref = rest[2 * P]
    m_scr, l_scr, acc_scr = rest[2 * P + 1:]
    p = pl.program_id(1)
    NR = H * T
    q = q_ref[0]
    nt = functools.partial(lax.dot_general, dimension_numbers=(((1,), (1,)), ((), ())),
                           preferred_element_type=F32)

    def masked(s, causal):
        row = lax.broadcasted_iota(jnp.int32, s.shape, 0) % NR
        col = lax.broadcasted_iota(jnp.int32, s.shape, 1)
        ok = col % H == row // T
        if causal:
            ok = jnp.logical_and(ok, col // H <= row % T)
        return jnp.where(ok, s, -jnp.inf)

    def update(s, v):
        m_old = m_scr[...]
        m_new = jnp.maximum(m_old, jnp.max(s, axis=-1, keepdims=True))
        corr = jnp.exp(m_old - m_new)
        e = jnp.exp(s - m_new)
        l_scr[...] = l_scr[...] * corr + jnp.sum(e, axis=-1, keepdims=True)
        m_scr[...] = m_new
        acc_scr[...] = acc_scr[...] * corr + jnp.dot(e.astype(BF16), v, preferred_element_type=F32)

    @pl.when(p == 0)
    def _():
        m_scr[...] = jnp.full(m_scr.shape, -jnp.inf, F32)
        l_scr[...] = jnp.zeros(l_scr.shape, F32)
        acc_scr[...] = jnp.zeros(acc_scr.shape, F32)
        kn = kn_ref[0]
        s = jnp.concatenate([nt(q[c], kn[c]) for c in range(2)], axis=0)
        update(masked(s, True), vn_ref[0])

    halves = [range(0, P // 2), range(P // 2, P)]
    scores = []
    for pages in halves:
        ss = []
        for c in range(2):
            kc = jnp.concatenate([k_refs[r][pl.ds(c, TP * H, stride=2), :] for r in pages], axis=0)
            ss.append(nt(q[c], kc.astype(BF16)))
        scores.append(masked(jnp.concatenate(ss, axis=0), False))
    for pages, s in zip(halves, scores):
        update(s, jnp.concatenate([v_refs[r][...] for r in pages], axis=0).astype(BF16))

    @pl.when(p == n_steps - 1)
    def _():
        lam = _diff_lambda_in_kernel(lam_ref, lam_init)
        po = acc_scr[...] / l_scr[...]
        o = po[:NR, :] - lam * po[NR:, :]
        o_ref[0] = _subln(o, g_ref, lam_init).astype(o_ref.dtype)


def paged_diff_attention(qs, kn, vn, k_pool, v_pool, layer, page_table, lam_rows, subln, H, T, lam_init):
    B = qs.shape[0]
    n_pages = page_table.shape[1]
    TP = k_pool.shape[2] // (2 * H)
    P = 8
    n_steps = n_pages // P
    W = 2 * LANES
    NR = H * T

    def kmap(r):
        return lambda b, p, pt: (layer, pt[b, p * P + r], 0, 0)

    in_specs = [pl.BlockSpec((1, 2, NR, LANES), lambda b, p, pt: (b, 0, 0, 0)),
                pl.BlockSpec((1, 2, NR, LANES), lambda b, p, pt: (b, 0, 0, 0)),
                pl.BlockSpec((1, NR, W), lambda b, p, pt: (b, 0, 0)),
                pl.BlockSpec((4, LANES), lambda b, p, pt: (0, 0)),
                pl.BlockSpec((1, W), lambda b, p, pt: (0, 0))]
    in_specs += [pl.BlockSpec((None, None, TP * 2 * H, LANES), kmap(r)) for r in range(P)]
    in_specs += [pl.BlockSpec((None, None, TP * H, W), kmap(r)) for r in range(P)]
    return pl.pallas_call(
        functools.partial(_paged_kernel, P=P, H=H, T=T, TP=TP, n_steps=n_steps, lam_init=lam_init),
        grid_spec=pltpu.PrefetchScalarGridSpec(
            num_scalar_prefetch=1,
            grid=(B, n_steps),
            in_specs=in_specs,
            out_specs=pl.BlockSpec((1, NR, W), lambda b, p, pt: (b, 0, 0)),
            scratch_shapes=[pltpu.VMEM((2 * NR, 1), F32), pltpu.VMEM((2 * NR, 1), F32),
                            pltpu.VMEM((2 * NR, W), F32)],
        ),
        out_shape=jax.ShapeDtypeStruct((B, NR, W), BF16),
        compiler_params=_cparams(("parallel", "arbitrary"), 48),
        name="paged_diff_attention",
    )(page_table, qs, kn, vn, lam_rows, subln.reshape(1, W), *([k_pool] * P), *([v_pool] * P))


def _mem_attn_kernel(q_ref, k_ref, v_ref, o_ref, *, H, scale):
    q = q_ref[0]
    k = k_ref[0].astype(BF16)
    v = v_ref[0].astype(BF16)
    for h in range(H):
        sl = slice(h * LANES, (h + 1) * LANES)
        s = lax.dot_general(q[:, sl], k[:, sl], (((1,), (1,)), ((), ())), preferred_element_type=F32) * scale
        m = jnp.max(s, axis=-1, keepdims=True)
        e = jnp.exp(s - m)
        p = e / jnp.sum(e, axis=-1, keepdims=True)
        o_ref[0, :, sl] = jnp.dot(p.astype(BF16), v[:, sl], preferred_element_type=F32).astype(o_ref.dtype)


def mem_attention(q, mk, mv, H):
    B, L, D = q.shape
    Mm = mk.shape[1]
    tq = min(L, 512)
    return pl.pallas_call(
        functools.partial(_mem_attn_kernel, H=H, scale=LANES ** -0.5),
        grid=(B, L // tq),
        in_specs=[pl.BlockSpec((1, tq, D), lambda b, i: (b, i, 0)),
                  pl.BlockSpec((1, Mm, D), lambda b, i: (b, 0, 0)),
                  pl.BlockSpec((1, Mm, D), lambda b, i: (b, 0, 0))],
        out_specs=pl.BlockSpec((1, tq, D), lambda b, i: (b, i, 0)),
        out_shape=jax.ShapeDtypeStruct((B, L, D), BF16),
        compiler_params=_cparams(("parallel", "parallel"), 24),
        name="mem_attention",
    )(q, mk, mv)


def _gdn_core(h, ba, B, L, conv_state, s0, j, conv_w, a_log, dt_bias, norm_g, dims):
    HK, HV, n_qk, n_conv = dims
    q_scale = LANES ** -0.5
    if conv_state is None:
        qkv = conv_prompt(h, conv_w, B, L, n_qk, n_conv, q_scale)
        new_conv = h.reshape(B, L, -1)[:, L - (CONV_W - 1):, :n_conv]
        NC = L // CHUNK
        zsrc, z_col0 = h, n_conv
        beta, gc, gct = gdn_gates(ba, a_log, dt_bias, CHUNK, HV)
    else:
        u = h[:, :n_conv].reshape(B, L, n_conv)
        ut = jnp.swapaxes(u, 0, 1)
        st = jnp.swapaxes(conv_state, 0, 1)
        y = conv_sample(ut, st, conv_w, n_qk, q_scale)
        new_conv = jnp.concatenate([conv_state, u], axis=1)[:, L:, :]
        padrows = lambda t: jnp.pad(t.reshape(B, L, -1), ((0, 0), (0, CHUNK - L), (0, 0))).reshape(B * CHUNK, -1)
        qkv = padrows(jnp.swapaxes(y, 0, 1))
        zsrc, z_col0 = padrows(h[:, n_conv:]), 0
        NC = 1
        beta, gc, gct = gdn_gates(padrows(ba), a_log, dt_bias, L, HV)
    og, s_new = gdn_recurrence(qkv, zsrc, z_col0, beta, gc, gct, s0, j if s0.shape[0] > 1 else 0, norm_g,
                               B, NC, HK, HV)
    if conv_state is not None:
        og = og.reshape(B, CHUNK, -1)[:, :L].reshape(B * L, -1)
    return og, new_conv, s_new


def _forward(groups, params):
    (norm_mix, gdn_w_in, gdn_conv_w, gdn_a_log, gdn_dt_bias, gdn_norm, gdn_w_out, attn_w_qkv,
     attn_lambda, attn_subln, attn_w_o, norm_xattn, w_xq, w_xo, norm_ffn, w_up, w_down, norm_final) = params
    depth = norm_mix.shape[0]
    HV = gdn_a_log.shape[1]
    HK = HV // 2
    n_qk = HK * LANES
    n_conv = gdn_conv_w.shape[2]
    XH = w_xq.shape[2] // LANES
    xs = [g["x"] for g in groups]
    tables = []
    for g in groups:
        cos2, sin2 = _rope_tables(g["pos"])
        if g["L"] < 256:
            cos2, sin2 = jnp.tile(cos2, (g["B"], 1)), jnp.tile(sin2, (g["B"], 1))
        tables.append((cos2, sin2))
    outs = [dict(conv=[], delta=[], k=[], v=[]) for _ in groups]

    def both(acts, w, layer, res=None, **kw):
        return [matmul(a, w, layer, res=None if res is None else res[n], **kw) for n, a in enumerate(acts)]

    for i in range(depth):
        j = i // 2
        hn = [rmsnorm(x, norm_mix[i], BF16) for x in xs]
        if i % 2 == 0:
            h = both(hn, gdn_w_in, j, n_out=n_conv + HV * LANES, w_is_t=True)
            og = []
            for g, o, h_g, hn_g in zip(groups, outs, h, hn):
                ba = matmul_tail(hn_g, gdn_w_in, j, n_conv + HV * LANES)
                conv_state, s0 = g["gdn_state"](j)
                og_g, new_conv, s_new = _gdn_core(h_g, ba, g["B"], g["L"], conv_state, s0, j, gdn_conv_w[j],
                                                  gdn_a_log[j], gdn_dt_bias[j], gdn_norm[j], (HK, HV, n_qk, n_conv))
                og.append(og_g)
                o["conv"].append(new_conv)
                o["delta"].append(s_new)
            xs = both(og, gdn_w_out, j, res=xs)
        else:
            lam_init = 0.8 - 0.6 * math.exp(-0.3 * i)
            n_qk_a = attn_w_qkv.shape[2] // 3
            qkv = both(hn, attn_w_qkv, j)
            oa = []
            for g, o, qkv_g, (cos2, sin2) in zip(groups, outs, qkv, tables):
                qb, kf, kb, vb = rope_split(qkv_g, cos2, sin2, n_qk_a, n_qk_a, g["q_scale"])
                oa.append(g["attn"](j, qb, kb, vb, attn_lambda[j], attn_subln[j], lam_init))
                o["k"].append(kf)
                o["v"].append(qkv_g)
            xs = both(oa, attn_w_o, j, res=xs)
        hx = [rmsnorm(x, norm_xattn[i], BF16) for x in xs]
        q = both(hx, w_xq, i, out_dtype=BF16)
        om = []
        for g, q_g in zip(groups, q):
            B, L = g["B"], g["L"]
            Lq = max(L, SLOT_ROWS)
            mk, mv = g["mem_kv"](i)
            q3 = jnp.pad(q_g.reshape(B, L, -1), ((0, 0), (0, Lq - L), (0, 0)))
            om.append(mem_attention(q3, mk, mv, XH)[:, :L].reshape(B * L, -1))
        xs = both(om, w_xo, i, res=xs)
        hf = [rmsnorm(x, norm_ffn[i], BF16) for x in xs]
        up = both(hf, w_up, i, out_dtype=BF16, act="relu2")
        xs = both(up, w_down, i, res=xs)
    ys = [rmsnorm(x, norm_final, F32) for x in xs]
    return ys, outs


def kernel(x_prompt, x_sample, mem_prompt, state_conv, state_delta, cache_k, cache_v, cache_mem_k, cache_mem_v, page_table, norm_mix, gdn_w_in, gdn_conv_w, gdn_a_log, gdn_dt_bias, gdn_norm, gdn_w_out, attn_w_qkv, attn_lambda, attn_subln, attn_w_o, norm_xattn, norm_mem, w_xq, w_xkv, w_xo, norm_ffn, w_up, w_down, norm_final):
    Bp, Lp, D = x_prompt.shape
    Bs, Ls, _ = x_sample.shape
    depth = norm_mix.shape[0]
    HV = gdn_a_log.shape[1]
    n_conv = gdn_conv_w.shape[2]
    n_attn, n_pool, TP, H = cache_k.shape[:4]
    W = 2 * LANES
    XH = w_xq.shape[2] // LANES
    Mm = mem_prompt.shape[1]
    past_len = page_table.shape[1] * TP

    gdn_w_in_t = jnp.swapaxes(gdn_w_in, 1, 2)
    params = (norm_mix, gdn_w_in_t, gdn_conv_w, gdn_a_log, gdn_dt_bias, gdn_norm, gdn_w_out, attn_w_qkv,
              attn_lambda, attn_subln, attn_w_o, norm_xattn, w_xq, w_xo, norm_ffn, w_up, w_down, norm_final)

    p_mk, p_mv = [], []

    def prompt_mem(i):
        mn = rmsnorm(mem_prompt.reshape(Bp * Mm, D), norm_mem[i], BF16)
        kv = matmul(mn, w_xkv, i).reshape(Bp, Mm, 2 * XH * LANES)
        mk, mv = kv[:, :, :XH * LANES], kv[:, :, XH * LANES:]
        p_mk.append(mk.reshape(Bp, Mm, XH, LANES))
        p_mv.append(mv.reshape(Bp, Mm, XH, LANES))
        return mk, mv

    def prompt_attn(j, qb, kb, vb, lam_rows, subln, lam_init):
        return flash_diff_attention(qb, kb, vb, lam_rows, subln, Bp, Lp, H, lam_init)

    zero_state = jnp.zeros((1, Bp, HV, LANES, LANES), F32)
    prompt = dict(x=x_prompt.reshape(Bp * Lp, D), B=Bp, L=Lp, pos=jnp.arange(Lp), mem_kv=prompt_mem,
                  gdn_state=lambda j: (None, zero_state), attn=prompt_attn,
                  q_scale=LANES ** -0.5 * math.log2(math.e))

    k_pool = cache_k.reshape(n_attn, n_pool, TP * H * 2, LANES)
    v_pool = cache_v.reshape(n_attn, n_pool, TP * H, W)

    def sample_attn(j, qb, kb, vb, lam_rows, subln, lam_init):
        qs = jnp.transpose(qb.reshape(Bs, Ls, H, 2, LANES), (0, 3, 2, 1, 4)).reshape(Bs, 2, H * Ls, LANES)
        kn = jnp.transpose(kb.reshape(Bs, Ls, H, 2, LANES), (0, 3, 1, 2, 4)).reshape(Bs, 2, Ls * H, LANES)
        oa = paged_diff_attention(qs, kn, vb.reshape(Bs, Ls * H, W), k_pool, v_pool, j, page_table,
                                  lam_rows, subln, H, Ls, lam_init)
        return jnp.swapaxes(oa.reshape(Bs, H, Ls, W), 1, 2).reshape(Bs * Ls, H * W)

    sample = dict(x=x_sample.reshape(Bs * Ls, D), B=Bs, L=Ls, pos=past_len + jnp.arange(Ls),
                  mem_kv=lambda i: (cache_mem_k[i].reshape(Bs, Mm, -1), cache_mem_v[i].reshape(Bs, Mm, -1)),
                  gdn_state=lambda j: (state_conv[j], state_delta), attn=sample_attn, q_scale=LANES ** -0.5)

    (yp, ys), (po, so) = _forward([prompt, sample], params)

    pk_all, pv_all = pack_kv(po["k"], po["v"], H)
    s_v = [t[:, t.shape[1] - H * W:] for t in so["v"]]
    return (yp.reshape(Bp, Lp, D), ys.reshape(Bs, Ls, D),
            jnp.stack(po["conv"]), jnp.stack(po["delta"]), jnp.stack(so["conv"]), jnp.stack(so["delta"]),
            pk_all.reshape(n_attn, Bp, Lp, H, 2, LANES), pv_all.reshape(n_attn, Bp, Lp, H, W),
            jnp.stack(so["k"]).reshape(n_attn, Bs, Ls, H, 2, LANES), jnp.stack(s_v).reshape(n_attn, Bs, Ls, H, W),
            jnp.stack(p_mk), jnp.stack(p_mv))
```
